```python
import jax, jax.numpy as jnp
from jax import lax
import numpy as np

D_MODEL = 1024
BATCH = 16
SEQ = 2048
DEPTH = 2

CHUNK = 64
EPS = 1e-6
POOL_WINDOWS = (2, 4, 8, 16)
POOL_GROUP = D_MODEL // 4
POOL_WIDTH = 4 * POOL_GROUP
SGU_LEN = 128
SGU_HEAD_DIM = 128
SGU_WIDTH = D_MODEL
SGU_HEADS = SGU_WIDTH // SGU_HEAD_DIM
GLA_HEADS = 4
GLA_KW = D_MODEL // 2
GLA_VW = D_MODEL
GLA_DK = GLA_KW // GLA_HEADS
GLA_DV = GLA_VW // GLA_HEADS
GLA_RANK = 16
GLA_TAU = 16.0
SB_HEAD_DIM = 128
SB_WIDTH = D_MODEL
SB_HEADS = SB_WIDTH // SB_HEAD_DIM
SB_QBLOCK = 128
EVEN_SIZES = (POOL_WIDTH, SGU_WIDTH, SGU_WIDTH, POOL_WIDTH + SGU_WIDTH)
ODD_SIZES = (GLA_KW, GLA_KW, GLA_VW, GLA_RANK, SB_WIDTH, SB_WIDTH, SB_WIDTH, GLA_VW + SB_WIDTH)
EVEN_IN = sum(EVEN_SIZES)
ODD_IN = sum(ODD_SIZES)
EVEN_MIX = POOL_WIDTH + SGU_WIDTH
ODD_MIX = GLA_VW + SB_WIDTH
N_EVEN = (DEPTH + 1) // 2
N_ODD = DEPTH // 2

kernel_name = "hybrid_pool_sgu_gla_stickbreak_adaln"


def _split(t, sizes):
    idx = [int(i) for i in np.cumsum(sizes)[:-1]]
    return jnp.split(t, idx, axis=-1)


def rmsnorm(x, g):
    xf = x.astype(jnp.float32)
    y = xf * lax.rsqrt(jnp.mean(xf * xf, axis=-1, keepdims=True) + EPS)
    return (y * g.astype(jnp.float32)).astype(x.dtype)


def ada_modulate(x, c, norm_g, ada_w, ada_b):
    ada = jax.nn.silu(c) @ ada_w + ada_b
    shift, scale, gate = jnp.split(ada, 3, axis=-1)
    h = rmsnorm(x, norm_g) * (1.0 + scale[:, None, :]) + shift[:, None, :]
    return h, gate[:, None, :]


def pool_mixer(a, pool_w, pool_scale):
    B_, S, _ = a.shape
    af = a.astype(jnp.float32)
    cs = jnp.concatenate([jnp.zeros_like(af[:, :1]), jnp.cumsum(af, axis=1)], axis=1)
    t = np.arange(S)
    outs = []
    for g, w in enumerate(POOL_WINDOWS):
        sl = slice(g * POOL_GROUP, (g + 1) * POOL_GROUP)
        start = np.maximum(t + 1 - w, 0)
        count = jnp.asarray((t + 1 - start).astype(np.float32))
        win_sum = cs[:, 1:, sl] - cs[:, start, sl]
        outs.append(win_sum / count[None, :, None] - af[..., sl])
    p = jnp.stack(outs, axis=2).astype(a.dtype)
    y = jnp.einsum('bsgc,gcd->bsgd', p, pool_w).reshape(B_, S, POOL_WIDTH)
    return y * pool_scale


def spatial_gating(u, v, norm_g, w_s, b_s):
    B_, S, _ = u.shape
    n = S // SGU_LEN
    vh = rmsnorm(v, norm_g).reshape(B_, n, SGU_LEN, SGU_HEADS, SGU_HEAD_DIM)
    mask = jnp.tril(jnp.ones((SGU_LEN, SGU_LEN), dtype=bool))
    w = jnp.where(mask[None], w_s, jnp.zeros_like(w_s))
    z = jnp.einsum('hts,bnshd->bnthd', w, vh) + b_s.T[None, None, :, :, None]
    return u * z.reshape(B_, S, SGU_WIDTH)


def gla(q, k, v, log_a):
    B_, S = q.shape[:2]
    n = S // CHUNK
    r = lambda t: t.astype(jnp.float32).reshape(B_, n, CHUNK, *t.shape[2:])
    qf = r(q) * (GLA_DK ** -0.5)
    kf, vf, la = r(k), r(v), r(log_a)
    bcum = jnp.cumsum(la, axis=2)
    b_last = bcum[:, :, -1]
    q_dec = qf * jnp.exp(bcum)
    k_inv = kf * jnp.exp(-bcum)
    k_end = kf * jnp.exp(b_last[:, :, None] - bcum)
    mask = jnp.tril(jnp.ones((CHUNK, CHUNK), dtype=bool))
    att = jnp.einsum('bnthk,bnshk->bnhts', q_dec, k_inv)
    att = jnp.where(mask, att, 0.0)
    o_intra = jnp.einsum('bnhts,bnshv->bnthv', att, vf)

    def step(state, xs):
        qd, ke, vc, bl = xs
        o = jnp.einsum('bthk,bhkv->bthv', qd, state)
        state = jnp.exp(bl)[..., None] * state + jnp.einsum('bshk,bshv->bhkv', ke, vc)
        return state, o

    s0 = jnp.zeros((B_, GLA_HEADS, GLA_DK, GLA_DV), jnp.float32)
    xs = (jnp.moveaxis(q_dec, 1, 0), jnp.moveaxis(k_end, 1, 0), jnp.moveaxis(vf, 1, 0), jnp.moveaxis(b_last, 1, 0))
    _, o_inter = lax.scan(step, s0, xs)
    o = o_intra + jnp.moveaxis(o_inter, 0, 1)
    return o.reshape(B_, S, GLA_HEADS, GLA_DV).astype(v.dtype)


def stick_breaking(q, k, v):
    B_, S, H, Dh = q.shape
    scale = Dh ** -0.5
    outs = []
    for i in range(S // SB_QBLOCK):
        q0, q1 = i * SB_QBLOCK, (i + 1) * SB_QBLOCK
        z = jnp.einsum('bthd,bshd->bhts', q[:, q0:q1], k[:, :q1]).astype(jnp.float32) * scale
        strict = np.arange(q1)[None, :] < np.arange(q0, q1)[:, None]
        log_beta = jax.nn.log_sigmoid(z)
        log_1m = jnp.where(strict, jax.nn.log_sigmoid(-z), 0.0)
        after = lax.cumsum(log_1m, axis=3, reverse=True) - log_1m
        w = jnp.where(strict, jnp.exp(log_beta + after), 0.0)
        outs.append(jnp.einsum('bhts,bshd->bthd', w.astype(v.dtype), v[:, :q1]))
    return jnp.concatenate(outs, axis=1)


def even_layer(x, c, norm_g, ada_w, ada_b, in_w, pool_w, pool_scale, sgu_norm_g, sgu_w, sgu_b, out_w):
    h, gate_res = ada_modulate(x, c, norm_g, ada_w, ada_b)
    a, u, v, gate = _split(h @ in_w, EVEN_SIZES)
    ya = pool_mixer(a, pool_w, pool_scale)
    yb = spatial_gating(u, v, sgu_norm_g, sgu_w, sgu_b)
    y = jnp.concatenate([ya, yb], axis=-1) * jax.nn.silu(gate)
    return x + gate_res * (y @ out_w)


def odd_layer(x, c, norm_g, ada_w, ada_b, in_w, gate_w, gate_b, gla_norm_g, out_w):
    B_, S, _ = x.shape
    h, gate_res = ada_modulate(x, c, norm_g, ada_w, ada_b)
    gq, gk, gv, glr, sq, sk, sv, gate = _split(h @ in_w, ODD_SIZES)
    log_a = jax.nn.log_sigmoid((glr @ gate_w + gate_b).astype(jnp.float32)) / GLA_TAU
    hd = lambda t, n, d: t.reshape(B_, S, n, d)
    yc = gla(hd(gq, GLA_HEADS, GLA_DK), hd(gk, GLA_HEADS, GLA_DK), hd(gv, GLA_HEADS, GLA_DV),
             hd(log_a, GLA_HEADS, GLA_DK))
    yc = rmsnorm(yc, gla_norm_g).reshape(B_, S, GLA_VW)
    yd = stick_breaking(hd(sq, SB_HEADS, SB_HEAD_DIM), hd(sk, SB_HEADS, SB_HEAD_DIM),
                        hd(sv, SB_HEADS, SB_HEAD_DIM)).reshape(B_, S, SB_WIDTH)
    y = jnp.concatenate([yc, yd], axis=-1) * jax.nn.silu(gate)
    return x + gate_res * (y @ out_w)


def setup_inputs(seed: int = 0) -> dict:
    key = jax.random.key(seed)
    ks = jax.random.split(key, 20)
    nrm = lambda k, shape, s: jax.random.normal(k, shape, jnp.float32) * s
    return {
        "x": nrm(ks[0], (BATCH, SEQ, D_MODEL), 1.0),
        "c": nrm(ks[1], (BATCH, D_MODEL), 1.0),
        "ada_w": nrm(ks[2], (DEPTH, D_MODEL, 3 * D_MODEL), 0.5 * D_MODEL ** -0.5),
        "ada_b": nrm(ks[3], (DEPTH, 3 * D_MODEL), 0.02),
        "norm_g": 1.0 + nrm(ks[4], (DEPTH, D_MODEL), 0.05),
        "even_in_w": nrm(ks[5], (N_EVEN, D_MODEL, EVEN_IN), D_MODEL ** -0.5),
        "pool_w": nrm(ks[6], (N_EVEN, 4, POOL_GROUP, POOL_GROUP), POOL_GROUP ** -0.5),
        "pool_scale": 1.0 + nrm(ks[7], (N_EVEN, POOL_WIDTH), 0.1),
        "sgu_norm_g": 1.0 + nrm(ks[8], (N_EVEN, SGU_WIDTH), 0.05),
        "sgu_w": nrm(ks[9], (N_EVEN, SGU_HEADS, SGU_LEN, SGU_LEN), SGU_LEN ** -0.5),
        "sgu_b": 1.0 + nrm(ks[10], (N_EVEN, SGU_HEADS, SGU_LEN), 0.1),
        "even_out_w": nrm(ks[11], (N_EVEN, EVEN_MIX, D_MODEL), EVEN_MIX ** -0.5),
        "odd_in_w": nrm(ks[12], (N_ODD, D_MODEL, ODD_IN), D_MODEL ** -0.5),
        "gla_gate_w": nrm(ks[13], (N_ODD, GLA_RANK, GLA_KW), GLA_RANK ** -0.5),
        "gla_gate_b": nrm(ks[14], (N_ODD, GLA_KW), 0.1),
        "gla_norm_g": 1.0 + nrm(ks[15], (N_ODD, GLA_HEADS, GLA_DV), 0.05),
        "odd_out_w": nrm(ks[16], (N_ODD, ODD_MIX, D_MODEL), ODD_MIX ** -0.5),
        "final_g": 1.0 + nrm(ks[17], (D_MODEL,), 0.05),
    }


def reference(x, c, ada_w, ada_b, norm_g, even_in_w, pool_w, pool_scale, sgu_norm_g, sgu_w, sgu_b,
              even_out_w, odd_in_w, gla_gate_w, gla_gate_b, gla_norm_g, odd_out_w, final_g):
    for i in range(DEPTH):
        j = i // 2
        if i % 2 == 0:
            x = even_layer(x, c, norm_g[i], ada_w[i], ada_b[i], even_in_w[j], pool_w[j], pool_scale[j],
                           sgu_norm_g[j], sgu_w[j], sgu_b[j], even_out_w[j])
        else:
            x = odd_layer(x, c, norm_g[i], ada_w[i], ada_b[i], odd_in_w[j], gla_gate_w[j], gla_gate_b[j],
                          gla_norm_g[j], odd_out_w[j])
    return rmsnorm(x, final_g)
```

```python
import functools

import jax
import jax.numpy as jnp
from jax import lax
from jax.experimental import pallas as pl
from jax.experimental.pallas import tpu as pltpu

F32 = jnp.float32
BF16 = jnp.bfloat16

EPS = 1e-6
POOL_WINDOWS = (2, 4, 8, 16)
POOL_GROUP = 256
POOL_HALO = 32
SGU_LEN = 128
SGU_HEADS = 8
GLA_HEADS = 4
GLA_DK = 128
GLA_DV = 256
GLA_CHUNK = 64
GLA_RANK = 16
GLA_RANK_PAD = 128
GLA_TAU = 16.0
SB_HEADS = 8
SB_DH = 128

TS_EVEN = 256
TS_ODD = 256
VMEM_LIMIT = 56 * 1024 * 1024


def _dot(a, b):
    return jnp.dot(a, b, preferred_element_type=F32)


def _dot_nt(a, b):
    return lax.dot_general(a, b, (((1,), (1,)), ((), ())), preferred_element_type=F32)


def _split2(x):
    hi = x.astype(BF16)
    lo = (x - hi.astype(F32)).astype(BF16)
    return hi, lo


def _sigmoid(x):
    return 1.0 / (1.0 + jnp.exp(-x))


def _silu(x):
    return x * _sigmoid(x)


def _softplus(x):
    return jnp.maximum(x, 0.0) + jnp.log(1.0 + jnp.exp(-jnp.abs(x)))


def _rms(x, g):
    return x * lax.rsqrt(jnp.mean(x * x, axis=-1, keepdims=True) + EPS) * g


def _resident(shape):
    n = len(shape)
    return pl.BlockSpec(shape, lambda *_: (0,) * n, pipeline_mode=pl.Buffered(1))


def _ada_kernel(c_ref, w_ref, b_ref, o_ref):
    sc = _silu(c_ref[...])
    o_ref[0] = jnp.dot(sc, w_ref[0], preferred_element_type=F32,
                       precision=lax.Precision.HIGHEST) + b_ref[0]


def _ada_call(c, ada_w, ada_b):
    depth, d, d3 = ada_w.shape
    b = c.shape[0]
    tn = 1024
    return pl.pallas_call(
        _ada_kernel,
        grid=(depth, d3 // tn),
        in_specs=[
            pl.BlockSpec((b, d), lambda i, j: (0, 0)),
            pl.BlockSpec((1, d, tn), lambda i, j: (i, 0, j)),
            pl.BlockSpec((1, 1, tn), lambda i, j: (i, 0, j)),
        ],
        out_specs=pl.BlockSpec((1, b, tn), lambda i, j: (i, 0, j)),
        out_shape=jax.ShapeDtypeStruct((depth, b, d3), F32),
        compiler_params=pltpu.CompilerParams(
            dimension_semantics=("arbitrary", "arbitrary"), vmem_limit_bytes=VMEM_LIMIT),
        name="ada",
    )(c, ada_w, ada_b.reshape(depth, 1, d3))


def _even_kernel(x_ref, ada_ref, ng_ref, win_ref, pw_ref, ps_ref, sg_ref, sw_ref, sbt_ref, wout_ref,
                 o_ref, abuf, s2buf, s4buf, s8buf, s16buf, *, ts):
    s = pl.program_id(1)
    d = x_ref.shape[-1]
    x = x_ref[0]
    shift = ada_ref[0, 0, 0:1, :]
    scale = ada_ref[0, 0, 1:2, :]
    gate_res = ada_ref[0, 0, 2:3, :]
    h = (_rms(x, ng_ref[...]) * (1.0 + scale) + shift).astype(BF16)

    hl = POOL_HALO

    @pl.when(s == 0)
    def _():
        abuf[0:hl, :] = jnp.zeros((hl, d), F32)

    @pl.when(s > 0)
    def _():
        abuf[0:hl, :] = abuf[ts:ts + hl, :]

    a = _dot(h, win_ref[:, 0:d])
    abuf[hl:hl + ts, :] = a
    g = POOL_GROUP
    n2 = ts + hl - 8
    s2buf[8:8 + n2, :] = abuf[8:8 + n2, :] + abuf[7:7 + n2, :]
    n4 = ts + hl - 16
    s4buf[16:16 + n4, :] = s2buf[16:16 + n4, g:] + s2buf[14:14 + n4, g:]
    n8 = ts + hl - 24
    s8buf[24:24 + n8, :] = s4buf[24:24 + n8, g:] + s4buf[20:20 + n8, g:]
    s16buf[hl:hl + ts, :] = s8buf[hl:hl + ts, g:] + s8buf[hl - 8:hl - 8 + ts, g:]

    t_glob = s * ts + lax.broadcasted_iota(jnp.int32, (ts, 1), 0)
    win_sums = (s2buf[hl:hl + ts, 0:g], s4buf[hl:hl + ts, 0:g], s8buf[hl:hl + ts, 0:g], s16buf[hl:hl + ts, :])
    ya = []
    for gi, w in enumerate(POOL_WINDOWS):
        count = jnp.minimum(t_glob + 1, w).astype(F32)
        p = win_sums[gi] / count - a[:, gi * g:(gi + 1) * g]
        ya.append(_dot(p.astype(BF16), pw_ref[gi]))
    ya = jnp.concatenate(ya, axis=1) * ps_ref[...]

    u = _dot(h, win_ref[:, d:2 * d])
    v = _dot(h, win_ref[:, 2 * d:3 * d])
    vn = _rms(v, sg_ref[...]).astype(BF16)
    nblk = ts // SGU_LEN
    row = lax.broadcasted_iota(jnp.int32, (SGU_LEN, SGU_LEN), 0)
    col = lax.broadcasted_iota(jnp.int32, (SGU_LEN, SGU_LEN), 1)
    hd = d // SGU_HEADS
    z_cols = []
    for hh in range(SGU_HEADS):
        wm = jnp.where(col <= row, sw_ref[hh], 0.0).astype(BF16)
        vh = jnp.concatenate(
            [vn[n * SGU_LEN:(n + 1) * SGU_LEN, hh * hd:(hh + 1) * hd] for n in range(nblk)], axis=1)
        zh = _dot(wm, vh) + sbt_ref[:, hh:hh + 1]
        z_cols.append(jnp.concatenate([zh[:, n * hd:(n + 1) * hd] for n in range(nblk)], axis=0))
    yb = u * jnp.concatenate(z_cols, axis=1)

    gate = _dot(h, win_ref[:, 3 * d:5 * d])
    y = (jnp.concatenate([ya, yb], axis=1) * _silu(gate)).astype(BF16)
    o_ref[0] = x + gate_res * _dot(y, wout_ref[...])


def _even_call(x, ada4, layer, norm_g, in_w, pool_w, pool_scale, sgu_norm_g, sgu_w, sgu_bt, out_w):
    b, sq, d = x.shape
    ts = TS_EVEN
    g = POOL_GROUP
    rows = ts + POOL_HALO
    return pl.pallas_call(
        functools.partial(_even_kernel, ts=ts),
        grid=(b, sq // ts),
        in_specs=[
            pl.BlockSpec((1, ts, d), lambda i, j: (i, j, 0)),
            pl.BlockSpec((1, 1, 3, d), lambda i, j: (layer, i, 0, 0)),
            _resident(norm_g.shape),
            _resident(in_w.shape),
            _resident(pool_w.shape),
            _resident(pool_scale.shape),
            _resident(sgu_norm_g.shape),
            _resident(sgu_w.shape),
            _resident(sgu_bt.shape),
            _resident(out_w.shape),
        ],
        out_specs=pl.BlockSpec((1, ts, d), lambda i, j: (i, j, 0)),
        out_shape=jax.ShapeDtypeStruct(x.shape, F32),
        scratch_shapes=[
            pltpu.VMEM((rows, d), F32),
            pltpu.VMEM((rows, d), F32),
            pltpu.VMEM((rows, d - g), F32),
            pltpu.VMEM((rows, d - 2 * g), F32),
            pltpu.VMEM((rows, d - 3 * g), F32),
        ],
        compiler_params=pltpu.CompilerParams(
            dimension_semantics=("arbitrary", "arbitrary"), vmem_limit_bytes=VMEM_LIMIT),
        name="even_layer",
    )(x, ada4, norm_g, in_w, pool_w, pool_scale, sgu_norm_g, sgu_w, sgu_bt, out_w)


def _odd_kernel(x_ref, ada_ref, ng_ref, win_ref, gw_ref, gb_ref, gng_ref, wout_ref, fg_ref,
                o_ref, state, kbuf, vbuf, qbuf, acc, cbuf, ybuf, *, ts):
    s = pl.program_id(1)
    d = x_ref.shape[-1]
    x = x_ref[0]
    shift = ada_ref[0, 0, 0:1, :]
    scale = ada_ref[0, 0, 1:2, :]
    gate_res = ada_ref[0, 0, 2:3, :]
    h = (_rms(x, ng_ref[...]) * (1.0 + scale) + shift).astype(BF16)

    kw = GLA_HEADS * GLA_DK
    vw = GLA_HEADS * GLA_DV
    sbw = SB_HEADS * SB_DH
    c_gq, c_gk, c_gv = 0, kw, 2 * kw
    c_sq = c_gv + vw
    c_sk = c_sq + sbw
    c_sv = c_sk + sbw
    c_gate = c_sv + sbw
    c_glr = c_gate + vw + sbw

    @pl.when(s == 0)
    def _():
        state[...] = jnp.zeros(state.shape, F32)

    glr = _dot(h, win_ref[:, c_glr:c_glr + GLA_RANK_PAD])
    glr_hi, glr_lo = _split2(glr)
    gw_hi, gw_lo = _split2(gw_ref[...])
    pre = _dot(glr_hi, gw_hi) + _dot(glr_lo, gw_hi) + _dot(glr_hi, gw_lo) + gb_ref[...]
    la = -_softplus(-pre) * (1.0 / GLA_TAU)
    r = lax.broadcasted_iota(jnp.int32, (ts, ts), 0)
    cc = lax.broadcasted_iota(jnp.int32, (ts, ts), 1)
    tri = jnp.where((cc <= r) & (cc // GLA_CHUNK == r // GLA_CHUNK), 1.0, 0.0).astype(BF16)
    la_hi, la_lo = _split2(la)
    la_lo2 = (la - la_hi.astype(F32) - la_lo.astype(F32)).astype(BF16)
    bcum = _dot(tri, la_hi) + _dot(tri, la_lo) + _dot(tri, la_lo2)

    gq = _dot(h, win_ref[:, c_gq:c_gq + kw]) * (GLA_DK ** -0.5)
    gk = _dot(h, win_ref[:, c_gk:c_gk + kw])
    gv = _dot(h, win_ref[:, c_gv:c_gv + vw])
    q_dec = (gq * jnp.exp(bcum)).astype(BF16)
    k_inv = (gk * jnp.exp(-bcum)).astype(BF16)
    gv_b = gv.astype(BF16)
    nchunk = ts // GLA_CHUNK
    cr = lax.broadcasted_iota(jnp.int32, (GLA_CHUNK, GLA_CHUNK), 0)
    ccol = lax.broadcasted_iota(jnp.int32, (GLA_CHUNK, GLA_CHUNK), 1)
    causal = ccol <= cr
    for ci in range(nchunk):
        r0 = ci * GLA_CHUNK
        b_last = bcum[r0 + GLA_CHUNK - 1:r0 + GLA_CHUNK, :]
        k_end = (gk[r0:r0 + GLA_CHUNK, :] * jnp.exp(b_last - bcum[r0:r0 + GLA_CHUNK, :])).astype(BF16)
        decay = jnp.exp(b_last)
        for hh in range(GLA_HEADS):
            ks = slice(hh * GLA_DK, (hh + 1) * GLA_DK)
            vs = slice(hh * GLA_DV, (hh + 1) * GLA_DV)
            qd = q_dec[r0:r0 + GLA_CHUNK, ks]
            att = jnp.where(causal, _dot_nt(qd, k_inv[r0:r0 + GLA_CHUNK, ks]), 0.0)
            vc = gv_b[r0:r0 + GLA_CHUNK, vs]
            st = state[hh]
            o = _dot(att.astype(BF16), vc) + _dot_nt(qd, st.astype(BF16))
            ybuf[r0:r0 + GLA_CHUNK, vs] = _rms(o, gng_ref[:, vs])
            vt = gv[r0:r0 + GLA_CHUNK, vs].T.astype(BF16)
            state[hh] = st * decay[:, ks] + _dot(vt, k_end[:, ks])

    sq = _dot(h, win_ref[:, c_sq:c_sq + sbw]).astype(BF16)
    sk = _dot(h, win_ref[:, c_sk:c_sk + sbw]).astype(BF16)
    sv = _dot(h, win_ref[:, c_sv:c_sv + sbw]).astype(BF16)
    s0 = pl.multiple_of(s * ts, ts)
    for hh in range(SB_HEADS):
        hs = slice(hh * SB_DH, (hh + 1) * SB_DH)
        qbuf[hh] = sq[:, hs]
        kbuf[hh, pl.ds(s0, ts), :] = sk[:, hs]
        vbuf[hh, pl.ds(s0, ts), :] = sv[:, hs]

    sb_scale = SB_DH ** -0.5
    jr = lax.broadcasted_iota(jnp.int32, (ts, ts), 0)
    jc = lax.broadcasted_iota(jnp.int32, (ts, ts), 1)
    later = jnp.where(jr > jc, 1.0, 0.0).astype(BF16)
    later2 = jnp.concatenate([later, later], axis=0)
    strict = jc < jr

    def sb_step(hh, k, v, carry, mask):
        z = _dot_nt(qbuf[hh], k) * sb_scale
        sp = _softplus(z)
        l1m = -sp
        lbeta = z - sp
        if mask is not None:
            l1m = jnp.where(mask, l1m, 0.0)
        hi, lo = _split2(l1m)
        after = _dot(jnp.concatenate([hi, lo], axis=1), later2)
        if carry is not None:
            after = after + jnp.concatenate([carry, carry], axis=1)
        w = jnp.exp(lbeta + after)
        if mask is not None:
            w = jnp.where(mask, w, 0.0)
        contrib = _dot(w.astype(BF16), v)
        rs = jnp.broadcast_to(jnp.sum(l1m, axis=1, keepdims=True), (ts, SB_DH))
        return contrib, rs

    for hh in range(SB_HEADS):
        contrib, rs = sb_step(hh, kbuf[hh, pl.ds(s0, ts), :], vbuf[hh, pl.ds(s0, ts), :], None, strict)
        acc[hh] = contrib
        cbuf[hh] = rs

    def key_block(i, _):
        j = s - 1 - i
        k0 = pl.multiple_of(j * ts, ts)
        for hh in range(SB_HEADS):
            carry = cbuf[hh]
            contrib, rs = sb_step(hh, kbuf[hh, pl.ds(k0, ts), :], vbuf[hh, pl.ds(k0, ts), :], carry, None)
            acc[hh] = acc[hh] + contrib
            cbuf[hh] = carry + rs
        return 0

    lax.fori_loop(0, s, key_block, 0)

    for hh in range(SB_HEADS):
        ybuf[:, vw + hh * SB_DH:vw + (hh + 1) * SB_DH] = acc[hh]

    gate = _dot(h, win_ref[:, c_gate:c_gate + vw + sbw])
    y = (ybuf[...] * _silu(gate)).astype(BF16)
    x1 = x + gate_res * _dot(y, wout_ref[...])
    o_ref[0] = _rms(x1, fg_ref[...])


def _odd_call(x, ada4, layer, norm_g, in_w, gate_w, gate_b, gla_norm_g, out_w, final_g):
    b, sq, d = x.shape
    ts = TS_ODD
    return pl.pallas_call(
        functools.partial(_odd_kernel, ts=ts),
        grid=(b, sq // ts),
        in_specs=[
            pl.BlockSpec((1, ts, d), lambda i, j: (i, j, 0)),
            pl.BlockSpec((1, 1, 3, d), lambda i, j: (layer, i, 0, 0)),
            _resident(norm_g.shape),
            _resident(in_w.shape),
            _resident(gate_w.shape),
            _resident(gate_b.shape),
            _resident(gla_norm_g.shape),
            _resident(out_w.shape),
            _resident(final_g.shape),
        ],
        out_specs=pl.BlockSpec((1, ts, d), lambda i, j: (i, j, 0)),
        out_shape=jax.ShapeDtypeStruct(x.shape, F32),
        scratch_shapes=[
            pltpu.VMEM((GLA_HEADS, GLA_DV, GLA_DK), F32),
            pltpu.VMEM((SB_HEADS, sq, SB_DH), BF16),
            pltpu.VMEM((SB_HEADS, sq, SB_DH), BF16),
            pltpu.VMEM((SB_HEADS, ts, SB_DH), BF16),
            pltpu.VMEM((SB_HEADS, ts, SB_DH), F32),
            pltpu.VMEM((SB_HEADS, ts, SB_DH), F32),
            pltpu.VMEM((ts, 2 * d), F32),
        ],
        compiler_params=pltpu.CompilerParams(
            dimension_semantics=("arbitrary", "arbitrary"), vmem_limit_bytes=VMEM_LIMIT),
        name="odd_layer",
    )(x, ada4, norm_g, in_w, gate_w, gate_b, gla_norm_g, out_w, final_g)


def kernel(x, c, ada_w, ada_b, norm_g, even_in_w, pool_w, pool_scale, sgu_norm_g, sgu_w, sgu_b, even_out_w,
           odd_in_w, gla_gate_w, gla_gate_b, gla_norm_g, odd_out_w, final_g):
    depth, d, _ = ada_w.shape
    assert depth == 2 and even_in_w.shape[0] == 1 and odd_in_w.shape[0] == 1
    b = x.shape[0]
    ada4 = _ada_call(c, ada_w, ada_b).reshape(depth, b, 3, d)

    x = _even_call(
        x, ada4, 0, norm_g[0:1], even_in_w[0].astype(BF16), pool_w[0].astype(BF16), pool_scale[0:1],
        sgu_norm_g[0:1], sgu_w[0], sgu_b[0].T, even_out_w[0].astype(BF16))

    w = odd_in_w[0]
    glr0 = 2 * GLA_HEADS * GLA_DK + GLA_HEADS * GLA_DV
    w_odd = jnp.concatenate(
        [w[:, :glr0], w[:, glr0 + GLA_RANK:], w[:, glr0:glr0 + GLA_RANK],
         jnp.zeros((d, GLA_RANK_PAD - GLA_RANK), w.dtype)], axis=1).astype(BF16)
    gate_w = jnp.concatenate(
        [gla_gate_w[0], jnp.zeros((GLA_RANK_PAD - GLA_RANK, gla_gate_w.shape[-1]), F32)], axis=0)
    return _odd_call(
        x, ada4, 1, norm_g[1:2], w_odd, gate_w, gla_gate_b[0:1], gla_norm_g[0].reshape(1, -1),
        odd_out_w[0].astype(BF16), final_g.reshape(1, -1))
```

```python
import functools

import jax
import jax.numpy as jnp
from jax import lax
from jax.experimental import pallas as pl
from jax.experimental.pallas import tpu as pltpu

F32 = jnp.float32
BF16 = jnp.bfloat16

EPS = 1e-6
POOL_WINDOWS = (2, 4, 8, 16)
POOL_GROUP = 256
POOL_HALO = 32
SGU_LEN = 128
SGU_HEADS = 8
GLA_HEADS = 4
GLA_DK = 128
GLA_DV = 256
GLA_CHUNK = 64
GLA_RANK = 16
GLA_RANK_PAD = 128
GLA_TAU = 16.0
SB_HEADS = 8
SB_DH = 128
SB_GROUP = 4

TS_EVEN = 256
TS_ODD = 256
VMEM_LIMIT = 56 * 1024 * 1024


def _dot(a, b):
    return jnp.dot(a, b, preferred_element_type=F32)


def _dot_nt(a, b):
    return lax.dot_general(a, b, (((1,), (1,)), ((), ())), preferred_element_type=F32)


def _split2(x):
    hi = x.astype(BF16)
    lo = (x - hi.astype(F32)).astype(BF16)
    return hi, lo


def _sigmoid(x):
    return 1.0 / (1.0 + jnp.exp(-x))


def _silu(x):
    return x * _sigmoid(x)


def _softplus(x):
    return jnp.maximum(x, 0.0) + jnp.log(1.0 + jnp.exp(-jnp.abs(x)))


def _rms(x, g):
    return x * lax.rsqrt(jnp.mean(x * x, axis=-1, keepdims=True) + EPS) * g


def _resident(shape):
    n = len(shape)
    return pl.BlockSpec(shape, lambda *_: (0,) * n, pipeline_mode=pl.Buffered(1))


def _ada_kernel(c_ref, w_ref, b_ref, o_ref):
    sc = _silu(c_ref[...])
    o_ref[0] = jnp.dot(sc, w_ref[0], preferred_element_type=F32,
                       precision=lax.Precision.HIGHEST) + b_ref[0]


def _ada_call(c, ada_w, ada_b):
    depth, d, d3 = ada_w.shape
    b = c.shape[0]
    tn = 1024
    return pl.pallas_call(
        _ada_kernel,
        grid=(depth, d3 // tn),
        in_specs=[
            pl.BlockSpec((b, d), lambda i, j: (0, 0)),
            pl.BlockSpec((1, d, tn), lambda i, j: (i, 0, j)),
            pl.BlockSpec((1, 1, tn), lambda i, j: (i, 0, j)),
        ],
        out_specs=pl.BlockSpec((1, b, tn), lambda i, j: (i, 0, j)),
        out_shape=jax.ShapeDtypeStruct((depth, b, d3), F32),
        compiler_params=pltpu.CompilerParams(
            dimension_semantics=("arbitrary", "arbitrary"), vmem_limit_bytes=VMEM_LIMIT),
        name="ada",
    )(c, ada_w, ada_b.reshape(depth, 1, d3))


def _even_kernel(x_ref, ada_ref, ng_ref, win_ref, pw_ref, ps_ref, sg_ref, sw_ref, sbt_ref, wout_ref,
                 o_ref, abuf, s2buf, s4buf, s8buf, s16buf, *, ts):
    s = pl.program_id(1)
    d = x_ref.shape[-1]
    x = x_ref[0]
    shift = ada_ref[0, 0, 0:1, :]
    scale = ada_ref[0, 0, 1:2, :]
    gate_res = ada_ref[0, 0, 2:3, :]
    h = (_rms(x, ng_ref[...]) * (1.0 + scale) + shift).astype(BF16)

    hl = POOL_HALO

    @pl.when(s == 0)
    def _():
        abuf[0:hl, :] = jnp.zeros((hl, d), F32)

    @pl.when(s > 0)
    def _():
        abuf[0:hl, :] = abuf[ts:ts + hl, :]

    a = _dot(h, win_ref[:, 0:d])
    abuf[hl:hl + ts, :] = a
    g = POOL_GROUP
    n2 = ts + hl - 8
    s2buf[8:8 + n2, :] = abuf[8:8 + n2, :] + abuf[7:7 + n2, :]
    n4 = ts + hl - 16
    s4buf[16:16 + n4, :] = s2buf[16:16 + n4, g:] + s2buf[14:14 + n4, g:]
    n8 = ts + hl - 24
    s8buf[24:24 + n8, :] = s4buf[24:24 + n8, g:] + s4buf[20:20 + n8, g:]
    s16buf[hl:hl + ts, :] = s8buf[hl:hl + ts, g:] + s8buf[hl - 8:hl - 8 + ts, g:]

    t_glob = s * ts + lax.broadcasted_iota(jnp.int32, (ts, 1), 0)
    win_sums = (s2buf[hl:hl + ts, 0:g], s4buf[hl:hl + ts, 0:g], s8buf[hl:hl + ts, 0:g], s16buf[hl:hl + ts, :])
    ya = []
    for gi, w in enumerate(POOL_WINDOWS):
        count = jnp.minimum(t_glob + 1, w).astype(F32)
        p = win_sums[gi] / count - a[:, gi * g:(gi + 1) * g]
        ya.append(_dot(p.astype(BF16), pw_ref[gi]))
    ya = jnp.concatenate(ya, axis=1) * ps_ref[...]

    u = _dot(h, win_ref[:, d:2 * d])
    v = _dot(h, win_ref[:, 2 * d:3 * d])
    vn = _rms(v, sg_ref[...]).astype(BF16)
    nblk = ts // SGU_LEN
    row = lax.broadcasted_iota(jnp.int32, (SGU_LEN, SGU_LEN), 0)
    col = lax.broadcasted_iota(jnp.int32, (SGU_LEN, SGU_LEN), 1)
    hd = d // SGU_HEADS
    z_cols = []
    for hh in range(SGU_HEADS):
        wm = jnp.where(col <= row, sw_ref[hh], 0.0).astype(BF16)
        vh = jnp.concatenate(
            [vn[n * SGU_LEN:(n + 1) * SGU_LEN, hh * hd:(hh + 1) * hd] for n in range(nblk)], axis=1)
        zh = _dot(wm, vh) + sbt_ref[:, hh:hh + 1]
        z_cols.append(jnp.concatenate([zh[:, n * hd:(n + 1) * hd] for n in range(nblk)], axis=0))
    yb = u * jnp.concatenate(z_cols, axis=1)

    gate = _dot(h, win_ref[:, 3 * d:5 * d])
    y = (jnp.concatenate([ya, yb], axis=1) * _silu(gate)).astype(BF16)
    o_ref[0] = x + gate_res * _dot(y, wout_ref[...])


def _even_call(x, ada4, layer, norm_g, in_w, pool_w, pool_scale, sgu_norm_g, sgu_w, sgu_bt, out_w):
    b, sq, d = x.shape
    ts = TS_EVEN
    g = POOL_GROUP
    rows = ts + POOL_HALO
    return pl.pallas_call(
        functools.partial(_even_kernel, ts=ts),
        grid=(b, sq // ts),
        in_specs=[
            pl.BlockSpec((1, ts, d), lambda i, j: (i, j, 0)),
            pl.BlockSpec((1, 1, 3, d), lambda i, j: (layer, i, 0, 0)),
            _resident(norm_g.shape),
            _resident(in_w.shape),
            _resident(pool_w.shape),
            _resident(pool_scale.shape),
            _resident(sgu_norm_g.shape),
            _resident(sgu_w.shape),
            _resident(sgu_bt.shape),
            _resident(out_w.shape),
        ],
        out_specs=pl.BlockSpec((1, ts, d), lambda i, j: (i, j, 0)),
        out_shape=jax.ShapeDtypeStruct(x.shape, F32),
        scratch_shapes=[
            pltpu.VMEM((rows, d), F32),
            pltpu.VMEM((rows, d), F32),
            pltpu.VMEM((rows, d - g), F32),
            pltpu.VMEM((rows, d - 2 * g), F32),
            pltpu.VMEM((rows, d - 3 * g), F32),
        ],
        compiler_params=pltpu.CompilerParams(
            dimension_semantics=("arbitrary", "arbitrary"), vmem_limit_bytes=VMEM_LIMIT),
        name="even_layer",
    )(x, ada4, norm_g, in_w, pool_w, pool_scale, sgu_norm_g, sgu_w, sgu_bt, out_w)


def _odd_kernel(x_ref, ada_ref, ng_ref, win_ref, gw_ref, gb_ref, gng_ref, wout_ref, fg_ref,
                o_ref, state, kbuf, vbuf, qbuf, acc, cbuf, ybuf, *, ts):
    s = pl.program_id(1)
    d = x_ref.shape[-1]
    x = x_ref[0]
    shift = ada_ref[0, 0, 0:1, :]
    scale = ada_ref[0, 0, 1:2, :]
    gate_res = ada_ref[0, 0, 2:3, :]
    h = (_rms(x, ng_ref[...]) * (1.0 + scale) + shift).astype(BF16)

    kw = GLA_HEADS * GLA_DK
    vw = GLA_HEADS * GLA_DV
    sbw = SB_HEADS * SB_DH
    c_gq, c_gk, c_gv = 0, kw, 2 * kw
    c_sq = c_gv + vw
    c_sk = c_sq + sbw
    c_sv = c_sk + sbw
    c_gate = c_sv + sbw
    c_glr = c_gate + vw + sbw

    @pl.when(s == 0)
    def _():
        state[...] = jnp.zeros(state.shape, F32)

    glr = _dot(h, win_ref[:, c_glr:c_glr + GLA_RANK_PAD])
    glr_hi, glr_lo = _split2(glr)
    gw_hi, gw_lo = _split2(gw_ref[...])
    pre = _dot(glr_hi, gw_hi) + _dot(glr_lo, gw_hi) + _dot(glr_hi, gw_lo) + gb_ref[...]
    la = -_softplus(-pre) * (1.0 / GLA_TAU)
    r = lax.broadcasted_iota(jnp.int32, (ts, ts), 0)
    cc = lax.broadcasted_iota(jnp.int32, (ts, ts), 1)
    same_chunk_causal = (cc <= r) & (cc // GLA_CHUNK == r // GLA_CHUNK)
    tri = jnp.where(same_chunk_causal, 1.0, 0.0).astype(BF16)
    la_hi, la_lo = _split2(la)
    la_lo2 = (la - la_hi.astype(F32) - la_lo.astype(F32)).astype(BF16)
    bcum = _dot(jnp.concatenate([tri, tri, tri], axis=1),
                jnp.concatenate([la_hi, la_lo, la_lo2], axis=0))

    nchunk = ts // GLA_CHUNK
    gq = _dot(h, win_ref[:, c_gq:c_gq + kw]) * (GLA_DK ** -0.5)
    gk = _dot(h, win_ref[:, c_gk:c_gk + kw])
    gv_b = _dot(h, win_ref[:, c_gv:c_gv + vw]).astype(BF16)
    q_dec = (gq * jnp.exp(bcum)).astype(BF16)
    k_inv = (gk * jnp.exp(-bcum)).astype(BF16)
    b_last = jnp.concatenate(
        [jnp.broadcast_to(bcum[(ci + 1) * GLA_CHUNK - 1:(ci + 1) * GLA_CHUNK, :], (GLA_CHUNK, kw))
         for ci in range(nchunk)], axis=0)
    k_end = gk * jnp.exp(b_last - bcum)

    heads = range(GLA_HEADS)
    kss = [slice(hh * GLA_DK, (hh + 1) * GLA_DK) for hh in heads]
    vss = [slice(hh * GLA_DV, (hh + 1) * GLA_DV) for hh in heads]
    zs = [_dot_nt(q_dec[:, kss[hh]], k_inv[:, kss[hh]]) for hh in heads]
    lane_chunk = lax.broadcasted_iota(jnp.int32, (GLA_DK, ts), 1) // GLA_CHUNK
    us, bcts = [], []
    for hh in heads:
        ket = k_end[:, kss[hh]].T
        bcts.append(bcum[:, kss[hh]].T)
        lhs = jnp.concatenate(
            [jnp.where(lane_chunk == ci, ket, 0.0).astype(BF16) for ci in range(nchunk)], axis=0)
        us.append(_dot(lhs, gv_b[:, vss[hh]]))
    zero_blk = jnp.zeros((GLA_CHUNK, GLA_DK), BF16)
    for hh in heads:
        att = jnp.where(same_chunk_causal, zs[hh], 0.0).astype(BF16)
        st = state[hh]
        starts = []
        for ci in range(nchunk):
            starts.append(st.astype(BF16))
            last = (ci + 1) * GLA_CHUNK - 1
            st = jnp.exp(bcts[hh][:, last:last + 1]) * st + us[hh][ci * GLA_DK:(ci + 1) * GLA_DK, :]
        state[hh] = st
        qh = q_dec[:, kss[hh]]
        q_blocks = jnp.concatenate(
            [jnp.concatenate([qh[ci * GLA_CHUNK:(ci + 1) * GLA_CHUNK, :] if cj == ci else zero_blk
                              for cj in range(nchunk)], axis=1) for ci in range(nchunk)], axis=0)
        o = _dot(jnp.concatenate([att, q_blocks], axis=1),
                 jnp.concatenate([gv_b[:, vss[hh]]] + starts, axis=0))
        ybuf[:, vss[hh]] = _rms(o, gng_ref[:, vss[hh]])

    sb_scale = SB_DH ** -0.5
    sq = (_dot(h, win_ref[:, c_sq:c_sq + sbw]) * sb_scale).astype(BF16)
    sk = _dot(h, win_ref[:, c_sk:c_sk + sbw]).astype(BF16)
    sv = _dot(h, win_ref[:, c_sv:c_sv + sbw]).astype(BF16)
    s0 = pl.multiple_of(s * ts, ts)
    for hh in range(SB_HEADS):
        hs = slice(hh * SB_DH, (hh + 1) * SB_DH)
        qbuf[hh] = sq[:, hs]
        kbuf[hh, pl.ds(s0, ts), :] = sk[:, hs]
        vbuf[hh, pl.ds(s0, ts), :] = sv[:, hs]

    jr = lax.broadcasted_iota(jnp.int32, (ts, ts), 0)
    jc = lax.broadcasted_iota(jnp.int32, (ts, ts), 1)
    incl = jnp.where(jr >= jc, 1.0, 0.0).astype(BF16)
    incl2 = jnp.concatenate([incl, incl], axis=0)
    strict = jc < jr

    def sb_heads(heads, k0, first):
        zs = [_dot_nt(qbuf[hh], kbuf[hh, pl.ds(k0, ts), :]) for hh in heads]
        hls = []
        for z in zs:
            sp = _softplus(z)
            if first:
                sp = jnp.where(strict, sp, 0.0)
            hls.append(jnp.concatenate(_split2(sp), axis=1))
        sums = [_dot(hl, incl2) for hl in hls]
        ws = []
        for hh, z, sm in zip(heads, zs, sums):
            xarg = z - sm
            if not first:
                carry = cbuf[hh]
                xarg = xarg - jnp.concatenate([carry, carry], axis=1)
            w = jnp.exp(xarg)
            if first:
                w = jnp.where(strict, w, 0.0)
            ws.append(w.astype(BF16))
        for hh, w, sm in zip(heads, ws, sums):
            contrib = _dot(w, vbuf[hh, pl.ds(k0, ts), :])
            rs = jnp.broadcast_to(sm[:, 0:1], (ts, SB_DH))
            if first:
                acc[hh] = contrib
                cbuf[hh] = rs
            else:
                acc[hh] = acc[hh] + contrib
                cbuf[hh] = cbuf[hh] + rs

    groups = [tuple(range(g, g + SB_GROUP)) for g in range(0, SB_HEADS, SB_GROUP)]
    for heads in groups:
        sb_heads(heads, s0, True)

    def key_block(i, _):
        k0 = pl.multiple_of((s - 1 - i) * ts, ts)
        for heads in groups:
            sb_heads(heads, k0, False)
        return 0

    lax.fori_loop(0, s, key_block, 0)

    for hh in range(SB_HEADS):
        ybuf[:, vw + hh * SB_DH:vw + (hh + 1) * SB_DH] = acc[hh]

    gate = _dot(h, win_ref[:, c_gate:c_gate + vw + sbw])
    y = (ybuf[...] * _silu(gate)).astype(BF16)
    x1 = x + gate_res * _dot(y, wout_ref[...])
    o_ref[0] = _rms(x1, fg_ref[...])


def _odd_call(x, ada4, layer, norm_g, in_w, gate_w, gate_b, gla_norm_g, out_w, final_g):
    b, sq, d = x.shape
    ts = TS_ODD
    return pl.pallas_call(
        functools.partial(_odd_kernel, ts=ts),
        grid=(b, sq // ts),
        in_specs=[
            pl.BlockSpec((1, ts, d), lambda i, j: (i, j, 0)),
            pl.BlockSpec((1, 1, 3, d), lambda i, j: (layer, i, 0, 0)),
            _resident(norm_g.shape),
            _resident(in_w.shape),
            _resident(gate_w.shape),
            _resident(gate_b.shape),
            _resident(gla_norm_g.shape),
            _resident(out_w.shape),
            _resident(final_g.shape),
        ],
        out_specs=pl.BlockSpec((1, ts, d), lambda i, j: (i, j, 0)),
        out_shape=jax.ShapeDtypeStruct(x.shape, F32),
        scratch_shapes=[
            pltpu.VMEM((GLA_HEADS, GLA_DK, GLA_DV), F32),
            pltpu.VMEM((SB_HEADS, sq, SB_DH), BF16),
            pltpu.VMEM((SB_HEADS, sq, SB_DH), BF16),
            pltpu.VMEM((SB_HEADS, ts, SB_DH), BF16),
            pltpu.VMEM((SB_HEADS, ts, SB_DH), F32),
            pltpu.VMEM((SB_HEADS, ts, SB_DH), F32),
            pltpu.VMEM((ts, 2 * d), F32),
        ],
        compiler_params=pltpu.CompilerParams(
            dimension_semantics=("arbitrary", "arbitrary"), vmem_limit_bytes=VMEM_LIMIT),
        name="odd_layer",
    )(x, ada4, norm_g, in_w, gate_w, gate_b, gla_norm_g, out_w, final_g)


def kernel(x, c, ada_w, ada_b, norm_g, even_in_w, pool_w, pool_scale, sgu_norm_g, sgu_w, sgu_b, even_out_w,
           odd_in_w, gla_gate_w, gla_gate_b, gla_norm_g, odd_out_w, final_g):
    depth, d, _ = ada_w.shape
    assert depth == 2 and even_in_w.shape[0] == 1 and odd_in_w.shape[0] == 1
    b = x.shape[0]
    ada4 = _ada_call(c, ada_w, ada_b).reshape(depth, b, 3, d)

    x = _even_call(
        x, ada4, 0, norm_g[0:1], even_in_w[0].astype(BF16), pool_w[0].astype(BF16), pool_scale[0:1],
        sgu_norm_g[0:1], sgu_w[0], sgu_b[0].T, even_out_w[0].astype(BF16))

    w = odd_in_w[0]
    glr0 = 2 * GLA_HEADS * GLA_DK + GLA_HEADS * GLA_DV
    w_odd = jnp.concatenate(
        [w[:, :glr0], w[:, glr0 + GLA_RANK:], w[:, glr0:glr0 + GLA_RANK],
         jnp.zeros((d, GLA_RANK_PAD - GLA_RANK), w.dtype)], axis=1).astype(BF16)
    gate_w = jnp.concatenate(
        [gla_gate_w[0], jnp.zeros((GLA_RANK_PAD - GLA_RANK, gla_gate_w.shape[-1]), F32)], axis=0)
    return _odd_call(
        x, ada4, 1, norm_g[1:2], w_odd, gate_w, gla_gate_b[0:1], gla_norm_g[0].reshape(1, -1),
        odd_out_w[0].astype(BF16), final_g.reshape(1, -1))
```

```python
import functools

import jax
import jax.numpy as jnp
from jax import lax
from jax.experimental import pallas as pl
from jax.experimental.pallas import tpu as pltpu

F32 = jnp.float32
BF16 = jnp.bfloat16

EPS = 1e-6
POOL_WINDOWS = (2, 4, 8, 16)
POOL_GROUP = 256
POOL_HALO = 32
SGU_LEN = 128
SGU_HEADS = 8
GLA_HEADS = 4
GLA_DK = 128
GLA_DV = 256
GLA_CHUNK = 64
GLA_RANK = 16
GLA_RANK_PAD = 128
GLA_TAU = 16.0
SB_HEADS = 8
SB_DH = 128
SB_GROUP = 4
SB_DEAD_CARRY = 105.0

TS_EVEN = 512
TS_ODD = 256
VMEM_LIMIT = 56 * 1024 * 1024


def _dot(a, b):
    return jnp.dot(a, b, preferred_element_type=F32)


def _dot_nt(a, b):
    return lax.dot_general(a, b, (((1,), (1,)), ((), ())), preferred_element_type=F32)


def _split2(x):
    hi = x.astype(BF16)
    lo = (x - hi.astype(F32)).astype(BF16)
    return hi, lo


def _sigmoid(x):
    return 1.0 / (1.0 + jnp.exp(-x))


def _silu(x):
    return x * _sigmoid(x)


def _softplus(x):
    return jnp.maximum(x, 0.0) + jnp.log(1.0 + jnp.exp(-jnp.abs(x)))


def _rms(x, g):
    return x * lax.rsqrt(jnp.mean(x * x, axis=-1, keepdims=True) + EPS) * g


def _resident(shape):
    n = len(shape)
    return pl.BlockSpec(shape, lambda *_: (0,) * n, pipeline_mode=pl.Buffered(1))


def _ada_kernel(c_ref, w_ref, b_ref, o_ref):
    sc = _silu(c_ref[...])
    o_ref[0] = jnp.dot(sc, w_ref[0], preferred_element_type=F32,
                       precision=lax.Precision.HIGHEST) + b_ref[0]


def _ada_call(c, ada_w, ada_b):
    depth, d, d3 = ada_w.shape
    b = c.shape[0]
    tn = 1024
    return pl.pallas_call(
        _ada_kernel,
        grid=(depth, d3 // tn),
        in_specs=[
            pl.BlockSpec((b, d), lambda i, j: (0, 0)),
            pl.BlockSpec((1, d, tn), lambda i, j: (i, 0, j)),
            pl.BlockSpec((1, 1, tn), lambda i, j: (i, 0, j)),
        ],
        out_specs=pl.BlockSpec((1, b, tn), lambda i, j: (i, 0, j)),
        out_shape=jax.ShapeDtypeStruct((depth, b, d3), F32),
        compiler_params=pltpu.CompilerParams(
            dimension_semantics=("arbitrary", "arbitrary"), vmem_limit_bytes=VMEM_LIMIT),
        name="ada",
    )(c, ada_w, ada_b.reshape(depth, 1, d3))


def _even_kernel(x_ref, ada_ref, ng_ref, win_ref, pw_ref, ps_ref, sg_ref, sw_ref, sbt_ref, wout_ref,
                 o_ref, abuf, s2buf, s4buf, s8buf, s16buf, *, ts):
    s = pl.program_id(1)
    d = x_ref.shape[-1]
    x = x_ref[0]
    shift = ada_ref[0, 0, 0:1, :]
    scale = ada_ref[0, 0, 1:2, :]
    gate_res = ada_ref[0, 0, 2:3, :]
    h = (_rms(x, ng_ref[...]) * (1.0 + scale) + shift).astype(BF16)

    hl = POOL_HALO

    @pl.when(s == 0)
    def _():
        abuf[0:hl, :] = jnp.zeros((hl, d), F32)

    @pl.when(s > 0)
    def _():
        abuf[0:hl, :] = abuf[ts:ts + hl, :]

    a = _dot(h, win_ref[:, 0:d])
    abuf[hl:hl + ts, :] = a
    g = POOL_GROUP
    n2 = ts + hl - 8
    s2buf[8:8 + n2, :] = abuf[8:8 + n2, :] + abuf[7:7 + n2, :]
    n4 = ts + hl - 16
    s4buf[16:16 + n4, :] = s2buf[16:16 + n4, g:] + s2buf[14:14 + n4, g:]
    n8 = ts + hl - 24
    s8buf[24:24 + n8, :] = s4buf[24:24 + n8, g:] + s4buf[20:20 + n8, g:]
    s16buf[hl:hl + ts, :] = s8buf[hl:hl + ts, g:] + s8buf[hl - 8:hl - 8 + ts, g:]

    t_glob = s * ts + lax.broadcasted_iota(jnp.int32, (ts, 1), 0)
    win_sums = (s2buf[hl:hl + ts, 0:g], s4buf[hl:hl + ts, 0:g], s8buf[hl:hl + ts, 0:g], s16buf[hl:hl + ts, :])
    ya = []
    for gi, w in enumerate(POOL_WINDOWS):
        count = jnp.minimum(t_glob + 1, w).astype(F32)
        p = win_sums[gi] / count - a[:, gi * g:(gi + 1) * g]
        ya.append(_dot(p.astype(BF16), pw_ref[gi]))
    ya = jnp.concatenate(ya, axis=1) * ps_ref[...]

    u = _dot(h, win_ref[:, d:2 * d])
    v = _dot(h, win_ref[:, 2 * d:3 * d])
    vn = _rms(v, sg_ref[...]).astype(BF16)
    nblk = ts // SGU_LEN
    row = lax.broadcasted_iota(jnp.int32, (SGU_LEN, SGU_LEN), 0)
    col = lax.broadcasted_iota(jnp.int32, (SGU_LEN, SGU_LEN), 1)
    hd = d // SGU_HEADS
    z_cols = []
    for hh in range(SGU_HEADS):
        wm = jnp.where(col <= row, sw_ref[hh], 0.0).astype(BF16)
        vh = jnp.concatenate(
            [vn[n * SGU_LEN:(n + 1) * SGU_LEN, hh * hd:(hh + 1) * hd] for n in range(nblk)], axis=1)
        zh = _dot(wm, vh) + sbt_ref[:, hh:hh + 1]
        z_cols.append(jnp.concatenate([zh[:, n * hd:(n + 1) * hd] for n in range(nblk)], axis=0))
    yb = u * jnp.concatenate(z_cols, axis=1)

    gate = _dot(h, win_ref[:, 3 * d:5 * d])
    y = (jnp.concatenate([ya, yb], axis=1) * _silu(gate)).astype(BF16)
    o_ref[0] = x + gate_res * _dot(y, wout_ref[...])


def _even_call(x, ada4, layer, norm_g, in_w, pool_w, pool_scale, sgu_norm_g, sgu_w, sgu_bt, out_w):
    b, sq, d = x.shape
    ts = TS_EVEN
    g = POOL_GROUP
    rows = ts + POOL_HALO
    return pl.pallas_call(
        functools.partial(_even_kernel, ts=ts),
        grid=(b, sq // ts),
        in_specs=[
            pl.BlockSpec((1, ts, d), lambda i, j: (i, j, 0)),
            pl.BlockSpec((1, 1, 3, d), lambda i, j: (layer, i, 0, 0)),
            _resident(norm_g.shape),
            _resident(in_w.shape),
            _resident(pool_w.shape),
            _resident(pool_scale.shape),
            _resident(sgu_norm_g.shape),
            _resident(sgu_w.shape),
            _resident(sgu_bt.shape),
            _resident(out_w.shape),
        ],
        out_specs=pl.BlockSpec((1, ts, d), lambda i, j: (i, j, 0)),
        out_shape=jax.ShapeDtypeStruct(x.shape, F32),
        scratch_shapes=[
            pltpu.VMEM((rows, d), F32),
            pltpu.VMEM((rows, d), F32),
            pltpu.VMEM((rows, d - g), F32),
            pltpu.VMEM((rows, d - 2 * g), F32),
            pltpu.VMEM((rows, d - 3 * g), F32),
        ],
        compiler_params=pltpu.CompilerParams(
            dimension_semantics=("arbitrary", "arbitrary"), vmem_limit_bytes=VMEM_LIMIT),
        name="even_layer",
    )(x, ada4, norm_g, in_w, pool_w, pool_scale, sgu_norm_g, sgu_w, sgu_bt, out_w)


def _odd_kernel(x_ref, ada_ref, ng_ref, win_ref, gw_ref, gb_ref, gng_ref, wout_ref, fg_ref,
                o_ref, state, kbuf, vbuf, qbuf, acc, cbuf, ybuf, *, ts):
    s = pl.program_id(1)
    d = x_ref.shape[-1]
    x = x_ref[0]
    shift = ada_ref[0, 0, 0:1, :]
    scale = ada_ref[0, 0, 1:2, :]
    gate_res = ada_ref[0, 0, 2:3, :]
    h = (_rms(x, ng_ref[...]) * (1.0 + scale) + shift).astype(BF16)

    kw = GLA_HEADS * GLA_DK
    vw = GLA_HEADS * GLA_DV
    sbw = SB_HEADS * SB_DH
    c_gq, c_gk, c_gv = 0, kw, 2 * kw
    c_sq = c_gv + vw
    c_sk = c_sq + sbw
    c_sv = c_sk + sbw
    c_gate = c_sv + sbw
    c_glr = c_gate + vw + sbw

    @pl.when(s == 0)
    def _():
        state[...] = jnp.zeros(state.shape, F32)

    glr = _dot(h, win_ref[:, c_glr:c_glr + GLA_RANK_PAD])
    glr_hi, glr_lo = _split2(glr)
    gw_hi, gw_lo = _split2(gw_ref[...])
    pre = _dot(glr_hi, gw_hi) + _dot(glr_lo, gw_hi) + _dot(glr_hi, gw_lo) + gb_ref[...]
    la = -_softplus(-pre) * (1.0 / GLA_TAU)
    r = lax.broadcasted_iota(jnp.int32, (ts, ts), 0)
    cc = lax.broadcasted_iota(jnp.int32, (ts, ts), 1)
    same_chunk_causal = (cc <= r) & (cc // GLA_CHUNK == r // GLA_CHUNK)
    tri = jnp.where(same_chunk_causal, 1.0, 0.0).astype(BF16)
    la_hi, la_lo = _split2(la)
    la_lo2 = (la - la_hi.astype(F32) - la_lo.astype(F32)).astype(BF16)
    bcum = _dot(jnp.concatenate([tri, tri, tri], axis=1),
                jnp.concatenate([la_hi, la_lo, la_lo2], axis=0))

    nchunk = ts // GLA_CHUNK
    gq = _dot(h, win_ref[:, c_gq:c_gq + kw]) * (GLA_DK ** -0.5)
    gk = _dot(h, win_ref[:, c_gk:c_gk + kw])
    gv_b = _dot(h, win_ref[:, c_gv:c_gv + vw]).astype(BF16)
    q_dec = (gq * jnp.exp(bcum)).astype(BF16)
    k_inv = (gk * jnp.exp(-bcum)).astype(BF16)
    b_last = jnp.concatenate(
        [jnp.broadcast_to(bcum[(ci + 1) * GLA_CHUNK - 1:(ci + 1) * GLA_CHUNK, :], (GLA_CHUNK, kw))
         for ci in range(nchunk)], axis=0)
    k_end = gk * jnp.exp(b_last - bcum)

    heads = range(GLA_HEADS)
    kss = [slice(hh * GLA_DK, (hh + 1) * GLA_DK) for hh in heads]
    vss = [slice(hh * GLA_DV, (hh + 1) * GLA_DV) for hh in heads]
    zs = [_dot_nt(q_dec[:, kss[hh]], k_inv[:, kss[hh]]) for hh in heads]
    lane_chunk = lax.broadcasted_iota(jnp.int32, (GLA_DK, ts), 1) // GLA_CHUNK
    us, bcts = [], []
    for hh in heads:
        ket = k_end[:, kss[hh]].T
        bcts.append(bcum[:, kss[hh]].T)
        lhs = jnp.concatenate(
            [jnp.where(lane_chunk == ci, ket, 0.0).astype(BF16) for ci in range(nchunk)], axis=0)
        us.append(_dot(lhs, gv_b[:, vss[hh]]))
    zero_blk = jnp.zeros((GLA_CHUNK, GLA_DK), BF16)
    for hh in heads:
        att = jnp.where(same_chunk_causal, zs[hh], 0.0).astype(BF16)
        st = state[hh]
        starts = []
        for ci in range(nchunk):
            starts.append(st.astype(BF16))
            last = (ci + 1) * GLA_CHUNK - 1
            st = jnp.exp(bcts[hh][:, last:last + 1]) * st + us[hh][ci * GLA_DK:(ci + 1) * GLA_DK, :]
        state[hh] = st
        qh = q_dec[:, kss[hh]]
        q_blocks = jnp.concatenate(
            [jnp.concatenate([qh[ci * GLA_CHUNK:(ci + 1) * GLA_CHUNK, :] if cj == ci else zero_blk
                              for cj in range(nchunk)], axis=1) for ci in range(nchunk)], axis=0)
        o = _dot(jnp.concatenate([att, q_blocks], axis=1),
                 jnp.concatenate([gv_b[:, vss[hh]]] + starts, axis=0))
        ybuf[:, vss[hh]] = _rms(o, gng_ref[:, vss[hh]])

    sb_scale = SB_DH ** -0.5
    sq = (_dot(h, win_ref[:, c_sq:c_sq + sbw]) * sb_scale).astype(BF16)
    sk = _dot(h, win_ref[:, c_sk:c_sk + sbw]).astype(BF16)
    sv = _dot(h, win_ref[:, c_sv:c_sv + sbw]).astype(BF16)
    s0 = pl.multiple_of(s * ts, ts)
    for hh in range(SB_HEADS):
        hs = slice(hh * SB_DH, (hh + 1) * SB_DH)
        qbuf[hh] = sq[:, hs]
        kbuf[hh, pl.ds(s0, ts), :] = sk[:, hs]
        vbuf[hh, pl.ds(s0, ts), :] = sv[:, hs]

    jr = lax.broadcasted_iota(jnp.int32, (ts, ts), 0)
    jc = lax.broadcasted_iota(jnp.int32, (ts, ts), 1)
    incl = jnp.where(jr >= jc, 1.0, 0.0).astype(BF16)
    incl2 = jnp.concatenate([incl, incl], axis=0)
    strict = jc < jr

    def sb_heads(heads, k0, first):
        zs = [_dot_nt(qbuf[hh], kbuf[hh, pl.ds(k0, ts), :]) for hh in heads]
        hls = []
        for z in zs:
            sp = _softplus(z)
            if first:
                sp = jnp.where(strict, sp, 0.0)
            hls.append(jnp.concatenate(_split2(sp), axis=1))
        sums = [_dot(hl, incl2) for hl in hls]
        ws = []
        for hh, z, sm in zip(heads, zs, sums):
            xarg = z - sm
            if not first:
                carry = cbuf[hh]
                xarg = xarg - jnp.concatenate([carry, carry], axis=1)
            w = jnp.exp(xarg)
            if first:
                w = jnp.where(strict, w, 0.0)
            ws.append(w.astype(BF16))
        for hh, w, sm in zip(heads, ws, sums):
            contrib = _dot(w, vbuf[hh, pl.ds(k0, ts), :])
            rs = jnp.broadcast_to(sm[:, 0:1], (ts, SB_DH))
            if first:
                acc[hh] = contrib
                cbuf[hh] = rs
            else:
                acc[hh] = acc[hh] + contrib
                cbuf[hh] = cbuf[hh] + rs

    groups = [tuple(range(g, g + SB_GROUP)) for g in range(0, SB_HEADS, SB_GROUP)]
    for heads in groups:
        sb_heads(heads, s0, True)

    def more_blocks(state_):
        i, alive = state_
        return jnp.logical_and(i < s, alive > 0)

    def key_block(state_):
        i, _ = state_
        k0 = pl.multiple_of((s - 1 - i) * ts, ts)
        for heads in groups:
            sb_heads(heads, k0, False)
        low = cbuf[0]
        for hh in range(1, SB_HEADS):
            low = jnp.minimum(low, cbuf[hh])
        return i + 1, (jnp.min(low) < SB_DEAD_CARRY).astype(jnp.int32)

    lax.while_loop(more_blocks, key_block, (jnp.int32(0), jnp.int32(1)))

    for hh in range(SB_HEADS):
        ybuf[:, vw + hh * SB_DH:vw + (hh + 1) * SB_DH] = acc[hh]

    gate = _dot(h, win_ref[:, c_gate:c_gate + vw + sbw])
    y = (ybuf[...] * _silu(gate)).astype(BF16)
    x1 = x + gate_res * _dot(y, wout_ref[...])
    o_ref[0] = _rms(x1, fg_ref[...])


def _odd_call(x, ada4, layer, norm_g, in_w, gate_w, gate_b, gla_norm_g, out_w, final_g):
    b, sq, d = x.shape
    ts = TS_ODD
    return pl.pallas_call(
        functools.partial(_odd_kernel, ts=ts),
        grid=(b, sq // ts),
        in_specs=[
            pl.BlockSpec((1, ts, d), lambda i, j: (i, j, 0)),
            pl.BlockSpec((1, 1, 3, d), lambda i, j: (layer, i, 0, 0)),
            _resident(norm_g.shape),
            _resident(in_w.shape),
            _resident(gate_w.shape),
            _resident(gate_b.shape),
            _resident(gla_norm_g.shape),
            _resident(out_w.shape),
            _resident(final_g.shape),
        ],
        out_specs=pl.BlockSpec((1, ts, d), lambda i, j: (i, j, 0)),
        out_shape=jax.ShapeDtypeStruct(x.shape, F32),
        scratch_shapes=[
            pltpu.VMEM((GLA_HEADS, GLA_DK, GLA_DV), F32),
            pltpu.VMEM((SB_HEADS, sq, SB_DH), BF16),
            pltpu.VMEM((SB_HEADS, sq, SB_DH), BF16),
            pltpu.VMEM((SB_HEADS, ts, SB_DH), BF16),
            pltpu.VMEM((SB_HEADS, ts, SB_DH), F32),
            pltpu.VMEM((SB_HEADS, ts, SB_DH), F32),
            pltpu.VMEM((ts, 2 * d), F32),
        ],
        compiler_params=pltpu.CompilerParams(
            dimension_semantics=("arbitrary", "arbitrary"), vmem_limit_bytes=VMEM_LIMIT),
        name="odd_layer",
    )(x, ada4, norm_g, in_w, gate_w, gate_b, gla_norm_g, out_w, final_g)


def kernel(x, c, ada_w, ada_b, norm_g, even_in_w, pool_w, pool_scale, sgu_norm_g, sgu_w, sgu_b, even_out_w,
           odd_in_w, gla_gate_w, gla_gate_b, gla_norm_g, odd_out_w, final_g):
    depth, d, _ = ada_w.shape
    assert depth == 2 and even_in_w.shape[0] == 1 and odd_in_w.shape[0] == 1
    b = x.shape[0]
    ada4 = _ada_call(c, ada_w, ada_b).reshape(depth, b, 3, d)

    x = _even_call(
        x, ada4, 0, norm_g[0:1], even_in_w[0].astype(BF16), pool_w[0].astype(BF16), pool_scale[0:1],
        sgu_norm_g[0:1], sgu_w[0], sgu_b[0].T, even_out_w[0].astype(BF16))

    w = odd_in_w[0].astype(BF16)
    glr0 = 2 * GLA_HEADS * GLA_DK + GLA_HEADS * GLA_DV
    w_odd = jnp.concatenate(
        [w[:, :glr0], w[:, glr0 + GLA_RANK:], w[:, glr0:glr0 + GLA_RANK],
         jnp.zeros((d, GLA_RANK_PAD - GLA_RANK), BF16)], axis=1)
    gate_w = jnp.concatenate(
        [gla_gate_w[0], jnp.zeros((GLA_RANK_PAD - GLA_RANK, gla_gate_w.shape[-1]), F32)], axis=0)
    return _odd_call(
        x, ada4, 1, norm_g[1:2], w_odd, gate_w, gla_gate_b[0:1], gla_norm_g[0].reshape(1, -1),
        odd_out_w[0].astype(BF16), final_g.reshape(1, -1))
```

```python
import functools

import jax
import jax.numpy as jnp
from jax import lax
from jax.experimental import pallas as pl
from jax.experimental.pallas import tpu as pltpu

F32 = jnp.float32
BF16 = jnp.bfloat16

EPS = 1e-6
POOL_WINDOWS = (2, 4, 8, 16)
POOL_GROUP = 256
POOL_HALO = 32
SGU_LEN = 128
SGU_HEADS = 8
GLA_HEADS = 4
GLA_DK = 128
GLA_DV = 256
GLA_CHUNK = 64
GLA_RANK = 16
GLA_RANK_PAD = 128
GLA_TAU = 16.0
SB_HEADS = 8
SB_DH = 128
SB_GROUP = 4
SB_STAGES = 4
SB_DEAD_CARRY = 105.0

TS_EVEN = 512
TS_ODD = 256
VMEM_LIMIT = 56 * 1024 * 1024


def _dot(a, b):
    return jnp.dot(a, b, preferred_element_type=F32)


def _dot_nt(a, b):
    return lax.dot_general(a, b, (((1,), (1,)), ((), ())), preferred_element_type=F32)


def _split2(x):
    hi = x.astype(BF16)
    lo = (x - hi.astype(F32)).astype(BF16)
    return hi, lo


def _sigmoid(x):
    return 1.0 / (1.0 + jnp.exp(-x))


def _silu(x):
    return x * _sigmoid(x)


def _softplus(x):
    return jnp.maximum(x, 0.0) + jnp.log(1.0 + jnp.exp(-jnp.abs(x)))


def _rms(x, g):
    return x * lax.rsqrt(jnp.mean(x * x, axis=-1, keepdims=True) + EPS) * g


def _resident(shape):
    n = len(shape)
    return pl.BlockSpec(shape, lambda *_: (0,) * n, pipeline_mode=pl.Buffered(1))


def _ada_kernel(c_ref, w_ref, b_ref, o_ref):
    sc = _silu(c_ref[...])
    o_ref[0] = jnp.dot(sc, w_ref[0], preferred_element_type=F32,
                       precision=lax.Precision.HIGHEST) + b_ref[0]


def _ada_call(c, ada_w, ada_b):
    depth, d, d3 = ada_w.shape
    b = c.shape[0]
    tn = 1024
    return pl.pallas_call(
        _ada_kernel,
        grid=(depth, d3 // tn),
        in_specs=[
            pl.BlockSpec((b, d), lambda i, j: (0, 0)),
            pl.BlockSpec((1, d, tn), lambda i, j: (i, 0, j)),
            pl.BlockSpec((1, 1, tn), lambda i, j: (i, 0, j)),
        ],
        out_specs=pl.BlockSpec((1, b, tn), lambda i, j: (i, 0, j)),
        out_shape=jax.ShapeDtypeStruct((depth, b, d3), F32),
        compiler_params=pltpu.CompilerParams(
            dimension_semantics=("arbitrary", "arbitrary"), vmem_limit_bytes=VMEM_LIMIT),
        name="ada",
    )(c, ada_w, ada_b.reshape(depth, 1, d3))


def _even_kernel(x_ref, ada_ref, ng_ref, win_ref, pw_ref, ps_ref, sg_ref, sw_ref, sbt_ref, wout_ref,
                 o_ref, abuf, s2buf, s4buf, s8buf, s16buf, *, ts):
    s = pl.program_id(1)
    d = x_ref.shape[-1]
    hl = POOL_HALO
    g = POOL_GROUP

    @pl.when(s == 0)
    def _():
        abuf[0:hl, :] = jnp.zeros((hl, d), F32)

    @pl.when(s > 0)
    def _():
        abuf[0:hl, :] = abuf[ts:ts + hl, :]

    shift = ada_ref[0, 0, 0:1, :]
    scale = ada_ref[0, 0, 1:2, :]
    h = (_rms(x_ref[0], ng_ref[...]) * (1.0 + scale) + shift).astype(BF16)

    a = _dot(h, win_ref[:, 0:d])
    abuf[hl:hl + ts, :] = a
    u = _dot(h, win_ref[:, d:2 * d])

    n2 = ts + hl - 8
    s2buf[8:8 + n2, :] = abuf[8:8 + n2, :] + abuf[7:7 + n2, :]
    n4 = ts + hl - 16
    s4buf[16:16 + n4, :] = s2buf[16:16 + n4, g:] + s2buf[14:14 + n4, g:]
    n8 = ts + hl - 24
    s8buf[24:24 + n8, :] = s4buf[24:24 + n8, g:] + s4buf[20:20 + n8, g:]
    s16buf[hl:hl + ts, :] = s8buf[hl:hl + ts, g:] + s8buf[hl - 8:hl - 8 + ts, g:]
    v = _dot(h, win_ref[:, 2 * d:3 * d])

    t_glob = s * ts + lax.broadcasted_iota(jnp.int32, (ts, 1), 0)
    win_sums = (s2buf[hl:hl + ts, 0:g], s4buf[hl:hl + ts, 0:g], s8buf[hl:hl + ts, 0:g], s16buf[hl:hl + ts, :])
    ya = []
    for gi, w in enumerate(POOL_WINDOWS):
        count = jnp.minimum(t_glob + 1, w).astype(F32)
        p = win_sums[gi] / count - a[:, gi * g:(gi + 1) * g]
        ya.append(_dot(p.astype(BF16), pw_ref[gi]))
    gate_a = _silu(_dot(h, win_ref[:, 3 * d:4 * d]))
    ya = jnp.concatenate(ya, axis=1) * ps_ref[...] * gate_a

    vn = _rms(v, sg_ref[...]).astype(BF16)
    gate_b = _silu(_dot(h, win_ref[:, 4 * d:5 * d]))
    nblk = ts // SGU_LEN
    row = lax.broadcasted_iota(jnp.int32, (SGU_LEN, SGU_LEN), 0)
    col = lax.broadcasted_iota(jnp.int32, (SGU_LEN, SGU_LEN), 1)
    hd = d // SGU_HEADS
    z_cols = []
    for hh in range(SGU_HEADS):
        wm = jnp.where(col <= row, sw_ref[hh], 0.0).astype(BF16)
        vh = jnp.concatenate(
            [vn[n * SGU_LEN:(n + 1) * SGU_LEN, hh * hd:(hh + 1) * hd] for n in range(nblk)], axis=1)
        zh = _dot(wm, vh) + sbt_ref[:, hh:hh + 1]
        z_cols.append(jnp.concatenate([zh[:, n * hd:(n + 1) * hd] for n in range(nblk)], axis=0))
    yb = u * jnp.concatenate(z_cols, axis=1) * gate_b

    y = jnp.concatenate([ya, yb], axis=1).astype(BF16)
    o_ref[0] = x_ref[0] + ada_ref[0, 0, 2:3, :] * _dot(y, wout_ref[...])


def _even_call(x, ada4, layer, norm_g, in_w, pool_w, pool_scale, sgu_norm_g, sgu_w, sgu_bt, out_w):
    b, sq, d = x.shape
    ts = TS_EVEN
    g = POOL_GROUP
    rows = ts + POOL_HALO
    return pl.pallas_call(
        functools.partial(_even_kernel, ts=ts),
        grid=(b, sq // ts),
        in_specs=[
            pl.BlockSpec((1, ts, d), lambda i, j: (i, j, 0)),
            pl.BlockSpec((1, 1, 3, d), lambda i, j: (layer, i, 0, 0)),
            _resident(norm_g.shape),
            _resident(in_w.shape),
            _resident(pool_w.shape),
            _resident(pool_scale.shape),
            _resident(sgu_norm_g.shape),
            _resident(sgu_w.shape),
            _resident(sgu_bt.shape),
            _resident(out_w.shape),
        ],
        out_specs=pl.BlockSpec((1, ts, d), lambda i, j: (i, j, 0)),
        out_shape=jax.ShapeDtypeStruct(x.shape, F32),
        scratch_shapes=[
            pltpu.VMEM((rows, d), F32),
            pltpu.VMEM((rows, d), F32),
            pltpu.VMEM((rows, d - g), F32),
            pltpu.VMEM((rows, d - 2 * g), F32),
            pltpu.VMEM((rows, d - 3 * g), F32),
        ],
        compiler_params=pltpu.CompilerParams(
            dimension_semantics=("arbitrary", "arbitrary"), vmem_limit_bytes=VMEM_LIMIT),
        name="even_layer",
    )(x, ada4, norm_g, in_w, pool_w, pool_scale, sgu_norm_g, sgu_w, sgu_bt, out_w)


def _odd_kernel(x_ref, ada_ref, ng_ref, wg_ref, ws_ref, wr_ref, gw_ref, gb_ref, gng_ref, wout_ref, fg_ref,
                o_ref, state, kbuf, vbuf, qbuf, acc, cbuf, ybuf, gbuf, *, ts):
    s = pl.program_id(1)

    @pl.when(s == 0)
    def _():
        state[...] = jnp.zeros(state.shape, F32)

    shift = ada_ref[0, 0, 0:1, :]
    scale = ada_ref[0, 0, 1:2, :]
    h = (_rms(x_ref[0], ng_ref[...]) * (1.0 + scale) + shift).astype(BF16)

    kw = GLA_HEADS * GLA_DK
    vw = GLA_HEADS * GLA_DV
    sbw = SB_HEADS * SB_DH
    c_gate = 3 * sbw
    sb_scale = SB_DH ** -0.5
    s0 = pl.multiple_of(s * ts, ts)

    glr = _dot(h, wr_ref[...])
    gq = _dot(h, wg_ref[:, 0:kw]) * (GLA_DK ** -0.5)
    glr_hi, glr_lo = _split2(glr)
    gw_hi, gw_lo = _split2(gw_ref[...])
    pre = _dot(glr_hi, gw_hi) + _dot(glr_lo, gw_hi) + _dot(glr_hi, gw_lo) + gb_ref[...]
    gk = _dot(h, wg_ref[:, kw:2 * kw])
    la = -_softplus(-pre) * (1.0 / GLA_TAU)
    gv_b = _dot(h, wg_ref[:, 2 * kw:2 * kw + vw]).astype(BF16)
    r = lax.broadcasted_iota(jnp.int32, (ts, ts), 0)
    cc = lax.broadcasted_iota(jnp.int32, (ts, ts), 1)
    same_chunk_causal = (cc <= r) & (cc // GLA_CHUNK == r // GLA_CHUNK)
    tri = jnp.where(same_chunk_causal, 1.0, 0.0).astype(BF16)
    la_hi, la_lo = _split2(la)
    la_lo2 = (la - la_hi.astype(F32) - la_lo.astype(F32)).astype(BF16)
    sq = (_dot(h, ws_ref[:, 0:sbw]) * sb_scale).astype(BF16)
    bcum = _dot(jnp.concatenate([tri, tri, tri], axis=1),
                jnp.concatenate([la_hi, la_lo, la_lo2], axis=0))
    sk = _dot(h, ws_ref[:, sbw:2 * sbw]).astype(BF16)
    sv = _dot(h, ws_ref[:, 2 * sbw:3 * sbw]).astype(BF16)
    for hh in range(SB_HEADS):
        hs = slice(hh * SB_DH, (hh + 1) * SB_DH)
        qbuf[hh] = sq[:, hs]
        kbuf[hh, pl.ds(s0, ts), :] = sk[:, hs]
        vbuf[hh, pl.ds(s0, ts), :] = sv[:, hs]

    nchunk = ts // GLA_CHUNK
    q_dec = (gq * jnp.exp(bcum)).astype(BF16)
    k_inv = (gk * jnp.exp(-bcum)).astype(BF16)
    b_last = jnp.concatenate(
        [jnp.broadcast_to(bcum[(ci + 1) * GLA_CHUNK - 1:(ci + 1) * GLA_CHUNK, :], (GLA_CHUNK, kw))
         for ci in range(nchunk)], axis=0)
    k_end = gk * jnp.exp(b_last - bcum)

    heads = range(GLA_HEADS)
    kss = [slice(hh * GLA_DK, (hh + 1) * GLA_DK) for hh in heads]
    vss = [slice(hh * GLA_DV, (hh + 1) * GLA_DV) for hh in heads]
    zs = [_dot_nt(q_dec[:, kss[hh]], k_inv[:, kss[hh]]) for hh in heads]
    lane_chunk = lax.broadcasted_iota(jnp.int32, (GLA_DK, ts), 1) // GLA_CHUNK
    us, bcts = [], []
    for hh in heads:
        ket = k_end[:, kss[hh]].T
        bcts.append(bcum[:, kss[hh]].T)
        lhs = jnp.concatenate(
            [jnp.where(lane_chunk == ci, ket, 0.0).astype(BF16) for ci in range(nchunk)], axis=0)
        us.append(_dot(lhs, gv_b[:, vss[hh]]))
    zero_blk = jnp.zeros((GLA_CHUNK, GLA_DK), BF16)
    for hh in heads:
        att = jnp.where(same_chunk_causal, zs[hh], 0.0).astype(BF16)
        st = state[hh]
        starts = []
        for ci in range(nchunk):
            starts.append(st.astype(BF16))
            last = (ci + 1) * GLA_CHUNK - 1
            st = jnp.exp(bcts[hh][:, last:last + 1]) * st + us[hh][ci * GLA_DK:(ci + 1) * GLA_DK, :]
        state[hh] = st
        qh = q_dec[:, kss[hh]]
        q_blocks = jnp.concatenate(
            [jnp.concatenate([qh[ci * GLA_CHUNK:(ci + 1) * GLA_CHUNK, :] if cj == ci else zero_blk
                              for cj in range(nchunk)], axis=1) for ci in range(nchunk)], axis=0)
        o = _dot(jnp.concatenate([att, q_blocks], axis=1),
                 jnp.concatenate([gv_b[:, vss[hh]]] + starts, axis=0))
        ybuf[:, vss[hh]] = _rms(o, gng_ref[:, vss[hh]])

    groups = [tuple(range(g, g + SB_GROUP)) for g in range(0, SB_HEADS, SB_GROUP)]
    gate_chunk = (vw + sbw) // (SB_STAGES * len(groups))

    def gate_filler(ci):
        def run():
            lo = ci * gate_chunk
            gbuf[:, lo:lo + gate_chunk] = _silu(_dot(h, ws_ref[:, c_gate + lo:c_gate + lo + gate_chunk]))
        return run

    fillers = [gate_filler(ci) for ci in range(SB_STAGES * len(groups))]

    jr = lax.broadcasted_iota(jnp.int32, (ts, ts), 0)
    jc = lax.broadcasted_iota(jnp.int32, (ts, ts), 1)
    incl = jnp.where(jr >= jc, 1.0, 0.0).astype(BF16)
    incl2 = jnp.concatenate([incl, incl], axis=0)
    strict = jc < jr

    def sb_heads(heads, k0, first, between=()):
        between = list(between)

        def stage_done():
            if between:
                between.pop(0)()

        zs = [_dot_nt(qbuf[hh], kbuf[hh, pl.ds(k0, ts), :]) for hh in heads]
        stage_done()
        hls = []
        for z in zs:
            sp = _softplus(z)
            if first:
                sp = jnp.where(strict, sp, 0.0)
            hls.append(jnp.concatenate(_split2(sp), axis=1))
        stage_done()
        sums = [_dot(hl, incl2) for hl in hls]
        stage_done()
        ws = []
        for hh, z, sm in zip(heads, zs, sums):
            xarg = z - sm
            if not first:
                carry = cbuf[hh]
                xarg = xarg - jnp.concatenate([carry, carry], axis=1)
            w = jnp.exp(xarg)
            if first:
                w = jnp.where(strict, w, 0.0)
            ws.append(w.astype(BF16))
        stage_done()
        for hh, w, sm in zip(heads, ws, sums):
            contrib = _dot(w, vbuf[hh, pl.ds(k0, ts), :])
            rs = jnp.broadcast_to(sm[:, 0:1], (ts, SB_DH))
            if first:
                acc[hh] = contrib
                cbuf[hh] = rs
            else:
                acc[hh] = acc[hh] + contrib
                cbuf[hh] = cbuf[hh] + rs

    for gi, heads in enumerate(groups):
        sb_heads(heads, s0, True, fillers[gi * SB_STAGES:(gi + 1) * SB_STAGES])

    def more_blocks(state_):
        i, alive = state_
        return jnp.logical_and(i < s, alive > 0)

    def key_block(state_):
        i, _ = state_
        k0 = pl.multiple_of((s - 1 - i) * ts, ts)
        for heads in groups:
            sb_heads(heads, k0, False)
        low = cbuf[0]
        for hh in range(1, SB_HEADS):
            low = jnp.minimum(low, cbuf[hh])
        return i + 1, (jnp.min(low) < SB_DEAD_CARRY).astype(jnp.int32)

    lax.while_loop(more_blocks, key_block, (jnp.int32(0), jnp.int32(1)))

    for hh in range(SB_HEADS):
        ybuf[:, vw + hh * SB_DH:vw + (hh + 1) * SB_DH] = acc[hh]

    y = (ybuf[...] * gbuf[...]).astype(BF16)
    x1 = x_ref[0] + ada_ref[0, 0, 2:3, :] * _dot(y, wout_ref[...])
    o_ref[0] = _rms(x1, fg_ref[...])


def _odd_call(x, ada4, layer, norm_g, w_gla, w_sb, w_rank, gate_w, gate_b, gla_norm_g, out_w, final_g):
    b, sq, d = x.shape
    ts = TS_ODD
    return pl.pallas_call(
        functools.partial(_odd_kernel, ts=ts),
        grid=(b, sq // ts),
        in_specs=[
            pl.BlockSpec((1, ts, d), lambda i, j: (i, j, 0)),
            pl.BlockSpec((1, 1, 3, d), lambda i, j: (layer, i, 0, 0)),
            _resident(norm_g.shape),
            _resident(w_gla.shape),
            _resident(w_sb.shape),
            _resident(w_rank.shape),
            _resident(gate_w.shape),
            _resident(gate_b.shape),
            _resident(gla_norm_g.shape),
            _resident(out_w.shape),
            _resident(final_g.shape),
        ],
        out_specs=pl.BlockSpec((1, ts, d), lambda i, j: (i, j, 0)),
        out_shape=jax.ShapeDtypeStruct(x.shape, F32),
        scratch_shapes=[
            pltpu.VMEM((GLA_HEADS, GLA_DK, GLA_DV), F32),
            pltpu.VMEM((SB_HEADS, sq, SB_DH), BF16),
            pltpu.VMEM((SB_HEADS, sq, SB_DH), BF16),
            pltpu.VMEM((SB_HEADS, ts, SB_DH), BF16),
            pltpu.VMEM((SB_HEADS, ts, SB_DH), F32),
            pltpu.VMEM((SB_HEADS, ts, SB_DH), F32),
            pltpu.VMEM((ts, 2 * d), F32),
            pltpu.VMEM((ts, 2 * d), F32),
        ],
        compiler_params=pltpu.CompilerParams(
            dimension_semantics=("arbitrary", "arbitrary"), vmem_limit_bytes=VMEM_LIMIT),
        name="odd_layer",
    )(x, ada4, norm_g, w_gla, w_sb, w_rank, gate_w, gate_b, gla_norm_g, out_w, final_g)


def kernel(x, c, ada_w, ada_b, norm_g, even_in_w, pool_w, pool_scale, sgu_norm_g, sgu_w, sgu_b, even_out_w,
           odd_in_w, gla_gate_w, gla_gate_b, gla_norm_g, odd_out_w, final_g):
    depth, d, _ = ada_w.shape
    assert depth == 2 and even_in_w.shape[0] == 1 and odd_in_w.shape[0] == 1
    b = x.shape[0]
    ada4 = _ada_call(c, ada_w, ada_b).reshape(depth, b, 3, d)

    x = _even_call(
        x, ada4, 0, norm_g[0:1], even_in_w[0].astype(BF16), pool_w[0].astype(BF16), pool_scale[0:1],
        sgu_norm_g[0:1], sgu_w[0], sgu_b[0].T, even_out_w[0].astype(BF16))

    w = odd_in_w[0]
    glr0 = 2 * GLA_HEADS * GLA_DK + GLA_HEADS * GLA_DV
    w_gla = w[:, :glr0].astype(BF16)
    w_sb = w[:, glr0 + GLA_RANK:].astype(BF16)
    w_rank = jnp.pad(w[:, glr0:glr0 + GLA_RANK], ((0, 0), (0, GLA_RANK_PAD - GLA_RANK))).astype(BF16)
    gate_w = jnp.pad(gla_gate_w[0], ((0, GLA_RANK_PAD - GLA_RANK), (0, 0)))
    return _odd_call(
        x, ada4, 1, norm_g[1:2], w_gla, w_sb, w_rank, gate_w, gla_gate_b[0:1], gla_norm_g[0].reshape(1, -1),
        odd_out_w[0].astype(BF16), final_g.reshape(1, -1))
```

```python
import functools

import jax
import jax.numpy as jnp
from jax import lax
from jax.experimental import pallas as pl
from jax.experimental.pallas import tpu as pltpu

F32 = jnp.float32
BF16 = jnp.bfloat16

EPS = 1e-6
LOG2E = 1.4426950408889634
POOL_WINDOWS = (2, 4, 8, 16)
POOL_GROUP = 256
POOL_HALO = 32
SGU_LEN = 128
SGU_HEADS = 8
GLA_HEADS = 4
GLA_DK = 128
GLA_DV = 256
GLA_CHUNK = 64
GLA_RANK = 16
GLA_RANK_PAD = 128
GLA_TAU = 16.0
SB_HEADS = 8
SB_DH = 128
SB_GROUP = 4
SB_STAGES = 4
SB_DEAD_CARRY = 105.0

TS_EVEN = 512
TS_ODD = 256
VMEM_LIMIT = 56 * 1024 * 1024


def _dot(a, b):
    return jnp.dot(a, b, preferred_element_type=F32)


def _dot_nt(a, b):
    return lax.dot_general(a, b, (((1,), (1,)), ((), ())), preferred_element_type=F32)


def _split2(x):
    hi = x.astype(BF16)
    lo = (x - hi.astype(F32)).astype(BF16)
    return hi, lo


def _sigmoid(x):
    return 1.0 / (1.0 + jnp.exp(-x))


def _silu(x):
    return x * _sigmoid(x)


def _softplus(x):
    return jnp.maximum(x, 0.0) + jnp.log(1.0 + jnp.exp(-jnp.abs(x)))


def _rms(x, g):
    return x * lax.rsqrt(jnp.mean(x * x, axis=-1, keepdims=True) + EPS) * g


def _resident(shape):
    n = len(shape)
    return pl.BlockSpec(shape, lambda *_: (0,) * n, pipeline_mode=pl.Buffered(1))


def _ada_kernel(c_ref, w_ref, b_ref, o_ref):
    sc = _silu(c_ref[...])
    o_ref[0] = jnp.dot(sc, w_ref[0], preferred_element_type=F32,
                       precision=lax.Precision.HIGHEST) + b_ref[0]


def _ada_call(c, ada_w, ada_b):
    depth, d, d3 = ada_w.shape
    b = c.shape[0]
    tn = 1024
    return pl.pallas_call(
        _ada_kernel,
        grid=(depth, d3 // tn),
        in_specs=[
            pl.BlockSpec((b, d), lambda i, j: (0, 0)),
            pl.BlockSpec((1, d, tn), lambda i, j: (i, 0, j)),
            pl.BlockSpec((1, 1, tn), lambda i, j: (i, 0, j)),
        ],
        out_specs=pl.BlockSpec((1, b, tn), lambda i, j: (i, 0, j)),
        out_shape=jax.ShapeDtypeStruct((depth, b, d3), F32),
        compiler_params=pltpu.CompilerParams(
            dimension_semantics=("arbitrary", "arbitrary"), vmem_limit_bytes=VMEM_LIMIT),
        name="ada",
    )(c, ada_w, ada_b.reshape(depth, 1, d3))


def _even_kernel(x_ref, ada_ref, ng_ref, win_ref, pw_ref, ps_ref, sg_ref, sw_ref, sbt_ref, wout_ref,
                 o_ref, abuf, s2buf, s4buf, s8buf, s16buf, *, ts):
    s = pl.program_id(1)
    d = x_ref.shape[-1]
    hl = POOL_HALO
    g = POOL_GROUP

    @pl.when(s == 0)
    def _():
        abuf[0:hl, :] = jnp.zeros((hl, d), F32)

    @pl.when(s > 0)
    def _():
        abuf[0:hl, :] = abuf[ts:ts + hl, :]

    shift = ada_ref[0, 0, 0:1, :]
    scale = ada_ref[0, 0, 1:2, :]
    h = (_rms(x_ref[0], ng_ref[...]) * (1.0 + scale) + shift).astype(BF16)

    a = _dot(h, win_ref[:, 0:d])
    abuf[hl:hl + ts, :] = a
    u = _dot(h, win_ref[:, d:2 * d])

    n2 = ts + hl - 8
    s2buf[8:8 + n2, :] = abuf[8:8 + n2, :] + abuf[7:7 + n2, :]
    n4 = ts + hl - 16
    s4buf[16:16 + n4, :] = s2buf[16:16 + n4, g:] + s2buf[14:14 + n4, g:]
    n8 = ts + hl - 24
    s8buf[24:24 + n8, :] = s4buf[24:24 + n8, g:] + s4buf[20:20 + n8, g:]
    s16buf[hl:hl + ts, :] = s8buf[hl:hl + ts, g:] + s8buf[hl - 8:hl - 8 + ts, g:]
    v = _dot(h, win_ref[:, 2 * d:3 * d])

    t_glob = s * ts + lax.broadcasted_iota(jnp.int32, (ts, 1), 0)
    win_sums = (s2buf[hl:hl + ts, 0:g], s4buf[hl:hl + ts, 0:g], s8buf[hl:hl + ts, 0:g], s16buf[hl:hl + ts, :])
    ya = []
    for gi, w in enumerate(POOL_WINDOWS):
        count = jnp.minimum(t_glob + 1, w).astype(F32)
        p = win_sums[gi] / count - a[:, gi * g:(gi + 1) * g]
        ya.append(_dot(p.astype(BF16), pw_ref[gi]))
    gate_a = _silu(_dot(h, win_ref[:, 3 * d:4 * d]))
    ya = jnp.concatenate(ya, axis=1) * ps_ref[...] * gate_a

    vn = _rms(v, sg_ref[...]).astype(BF16)
    gate_b = _silu(_dot(h, win_ref[:, 4 * d:5 * d]))
    nblk = ts // SGU_LEN
    row = lax.broadcasted_iota(jnp.int32, (SGU_LEN, SGU_LEN), 0)
    col = lax.broadcasted_iota(jnp.int32, (SGU_LEN, SGU_LEN), 1)
    hd = d // SGU_HEADS
    z_cols = []
    for hh in range(SGU_HEADS):
        wm = jnp.where(col <= row, sw_ref[hh], 0.0).astype(BF16)
        vh = jnp.concatenate(
            [vn[n * SGU_LEN:(n + 1) * SGU_LEN, hh * hd:(hh + 1) * hd] for n in range(nblk)], axis=1)
        zh = _dot(wm, vh) + sbt_ref[:, hh:hh + 1]
        z_cols.append(jnp.concatenate([zh[:, n * hd:(n + 1) * hd] for n in range(nblk)], axis=0))
    yb = u * jnp.concatenate(z_cols, axis=1) * gate_b

    y = jnp.concatenate([ya, yb], axis=1).astype(BF16)
    o_ref[0] = x_ref[0] + ada_ref[0, 0, 2:3, :] * _dot(y, wout_ref[...])


def _even_call(x, ada4, layer, norm_g, in_w, pool_w, pool_scale, sgu_norm_g, sgu_w, sgu_bt, out_w):
    b, sq, d = x.shape
    ts = TS_EVEN
    g = POOL_GROUP
    rows = ts + POOL_HALO
    return pl.pallas_call(
        functools.partial(_even_kernel, ts=ts),
        grid=(b, sq // ts),
        in_specs=[
            pl.BlockSpec((1, ts, d), lambda i, j: (i, j, 0)),
            pl.BlockSpec((1, 1, 3, d), lambda i, j: (layer, i, 0, 0)),
            _resident(norm_g.shape),
            _resident(in_w.shape),
            _resident(pool_w.shape),
            _resident(pool_scale.shape),
            _resident(sgu_norm_g.shape),
            _resident(sgu_w.shape),
            _resident(sgu_bt.shape),
            _resident(out_w.shape),
        ],
        out_specs=pl.BlockSpec((1, ts, d), lambda i, j: (i, j, 0)),
        out_shape=jax.ShapeDtypeStruct(x.shape, F32),
        scratch_shapes=[
            pltpu.VMEM((rows, d), F32),
            pltpu.VMEM((rows, d), F32),
            pltpu.VMEM((rows, d - g), F32),
            pltpu.VMEM((rows, d - 2 * g), F32),
            pltpu.VMEM((rows, d - 3 * g), F32),
        ],
        compiler_params=pltpu.CompilerParams(
            dimension_semantics=("arbitrary", "arbitrary"), vmem_limit_bytes=VMEM_LIMIT),
        name="even_layer",
    )(x, ada4, norm_g, in_w, pool_w, pool_scale, sgu_norm_g, sgu_w, sgu_bt, out_w)


def _odd_kernel(x_ref, ada_ref, ng_ref, wg_ref, ws_ref, wr_ref, gw_ref, gb_ref, gng_ref, wout_ref, fg_ref,
                o_ref, state, kbuf, vtbuf, qtbuf, acc, crow, ybuf, gbuf, *, ts):
    s = pl.program_id(1)

    @pl.when(s == 0)
    def _():
        state[...] = jnp.zeros(state.shape, F32)

    shift = ada_ref[0, 0, 0:1, :]
    scale = ada_ref[0, 0, 1:2, :]
    h = (_rms(x_ref[0], ng_ref[...]) * (1.0 + scale) + shift).astype(BF16)

    kw = GLA_HEADS * GLA_DK
    vw = GLA_HEADS * GLA_DV
    sbw = SB_HEADS * SB_DH
    c_gate = 3 * sbw
    sb_scale = SB_DH ** -0.5
    s0 = pl.multiple_of(s * ts, ts)

    glr = _dot(h, wr_ref[...])
    gq = _dot(h, wg_ref[:, 0:kw]) * (GLA_DK ** -0.5)
    glr_hi, glr_lo = _split2(glr)
    gw_hi, gw_lo = _split2(gw_ref[...])
    pre = _dot(jnp.concatenate([glr_hi, glr_lo, glr_hi], axis=1),
               jnp.concatenate([gw_hi, gw_hi, gw_lo], axis=0)) + gb_ref[...]
    gk = _dot(h, wg_ref[:, kw:2 * kw])
    la = -_softplus(-pre) * (1.0 / GLA_TAU)
    gv_b = _dot(h, wg_ref[:, 2 * kw:2 * kw + vw]).astype(BF16)
    r = lax.broadcasted_iota(jnp.int32, (ts, ts), 0)
    cc = lax.broadcasted_iota(jnp.int32, (ts, ts), 1)
    same_chunk_causal = (cc <= r) & (cc // GLA_CHUNK == r // GLA_CHUNK)
    tri = jnp.where(same_chunk_causal, 1.0, 0.0).astype(BF16)
    la_hi, la_lo = _split2(la)
    sq = _dot(h, ws_ref[:, 0:sbw]) * sb_scale
    bcum = _dot(jnp.concatenate([tri, tri], axis=1), jnp.concatenate([la_hi, la_lo], axis=0))
    sk = _dot(h, ws_ref[:, sbw:2 * sbw]).astype(BF16)
    sv = _dot(h, ws_ref[:, 2 * sbw:3 * sbw])
    for hh in range(SB_HEADS):
        hs = slice(hh * SB_DH, (hh + 1) * SB_DH)
        qtbuf[hh] = sq[:, hs].T.astype(BF16)
        kbuf[hh, pl.ds(s0, ts), :] = sk[:, hs]
        vtbuf[hh, s] = sv[:, hs].T.astype(BF16)

    nchunk = ts // GLA_CHUNK
    q_dec = (gq * jnp.exp(bcum)).astype(BF16)
    k_inv = (gk * jnp.exp(-bcum)).astype(BF16)
    b_last = jnp.concatenate(
        [jnp.broadcast_to(bcum[(ci + 1) * GLA_CHUNK - 1:(ci + 1) * GLA_CHUNK, :], (GLA_CHUNK, kw))
         for ci in range(nchunk)], axis=0)
    k_end = gk * jnp.exp(b_last - bcum)

    heads = range(GLA_HEADS)
    kss = [slice(hh * GLA_DK, (hh + 1) * GLA_DK) for hh in heads]
    vss = [slice(hh * GLA_DV, (hh + 1) * GLA_DV) for hh in heads]
    zs = [_dot_nt(q_dec[:, kss[hh]], k_inv[:, kss[hh]]) for hh in heads]
    lane_chunk = lax.broadcasted_iota(jnp.int32, (GLA_DK, ts), 1) // GLA_CHUNK
    us, bcts = [], []
    for hh in heads:
        ket = k_end[:, kss[hh]].T
        bcts.append(bcum[:, kss[hh]].T)
        lhs = jnp.concatenate(
            [jnp.where(lane_chunk == ci, ket, 0.0).astype(BF16) for ci in range(nchunk)], axis=0)
        us.append(_dot(lhs, gv_b[:, vss[hh]]))
    zero_blk = jnp.zeros((GLA_CHUNK, GLA_DK), BF16)
    for hh in heads:
        att = jnp.where(same_chunk_causal, zs[hh], 0.0).astype(BF16)
        st = state[hh]
        starts = []
        for ci in range(nchunk):
            starts.append(st.astype(BF16))
            last = (ci + 1) * GLA_CHUNK - 1
            st = jnp.exp(bcts[hh][:, last:last + 1]) * st + us[hh][ci * GLA_DK:(ci + 1) * GLA_DK, :]
        state[hh] = st
        qh = q_dec[:, kss[hh]]
        q_blocks = jnp.concatenate(
            [jnp.concatenate([qh[ci * GLA_CHUNK:(ci + 1) * GLA_CHUNK, :] if cj == ci else zero_blk
                              for cj in range(nchunk)], axis=1) for ci in range(nchunk)], axis=0)
        o = _dot(jnp.concatenate([att, q_blocks], axis=1),
                 jnp.concatenate([gv_b[:, vss[hh]]] + starts, axis=0))
        ybuf[:, vss[hh]] = _rms(o, gng_ref[:, vss[hh]])

    groups = [tuple(range(g, g + SB_GROUP)) for g in range(0, SB_HEADS, SB_GROUP)]
    gate_chunk = (vw + sbw) // (SB_STAGES * len(groups))

    def gate_filler(ci):
        def run():
            lo = ci * gate_chunk
            gbuf[:, lo:lo + gate_chunk] = _silu(_dot(h, ws_ref[:, c_gate + lo:c_gate + lo + gate_chunk]))
        return run

    fillers = [gate_filler(ci) for ci in range(SB_STAGES * len(groups))]

    jr = lax.broadcasted_iota(jnp.int32, (ts, ts), 0)
    jc = lax.broadcasted_iota(jnp.int32, (ts, ts), 1)
    later = jnp.where(jc > jr, 1.0, 0.0).astype(BF16)
    valid = jr < jc

    def sb_heads(heads, kb, first, between=()):
        between = list(between)

        def stage_done():
            if between:
                between.pop(0)()

        k0 = pl.multiple_of(kb * ts, ts)
        zs = [_dot(kbuf[hh, pl.ds(k0, ts), :], qtbuf[hh]) for hh in heads]
        stage_done()
        sps, lbetas, firsts = [], [], []
        for z in zs:
            lg = jnp.log(1.0 + jnp.exp2(jnp.abs(z) * (-LOG2E)))
            sp = jnp.maximum(z, 0.0) + lg
            if first:
                sp = jnp.where(valid, sp, 0.0)
            sps.append(sp.astype(BF16))
            firsts.append(sp[0:1, :])
            lbetas.append(jnp.minimum(z, 0.0) - lg)
        stage_done()
        sums = [_dot(later, sp) for sp in sps]
        stage_done()
        ws = []
        for hh, lbeta, sm in zip(heads, lbetas, sums):
            xarg = lbeta - sm
            if not first:
                xarg = xarg - crow[hh]
            w = jnp.exp(xarg)
            if first:
                w = jnp.where(valid, w, 0.0)
            ws.append(w.astype(BF16))
        stage_done()
        for hh, w, sm, sp0 in zip(heads, ws, sums, firsts):
            contrib = _dot(vtbuf[hh, kb], w)
            total = sm[0:1, :] + sp0
            if first:
                acc[hh] = contrib
                crow[hh] = total
            else:
                acc[hh] = acc[hh] + contrib
                crow[hh] = crow[hh] + total

    for gi, heads in enumerate(groups):
        sb_heads(heads, s, True, fillers[gi * SB_STAGES:(gi + 1) * SB_STAGES])

    def more_blocks(state_):
        i, alive = state_
        return jnp.logical_and(i < s, alive > 0)

    def key_block(state_):
        i, _ = state_
        for heads in groups:
            sb_heads(heads, s - 1 - i, False)
        low = crow[0]
        for hh in range(1, SB_HEADS):
            low = jnp.minimum(low, crow[hh])
        return i + 1, (jnp.min(low) < SB_DEAD_CARRY).astype(jnp.int32)

    lax.while_loop(more_blocks, key_block, (jnp.int32(0), jnp.int32(1)))

    for hh in range(SB_HEADS):
        ybuf[:, vw + hh * SB_DH:vw + (hh + 1) * SB_DH] = acc[hh].T

    y = (ybuf[...] * gbuf[...]).astype(BF16)
    x1 = x_ref[0] + ada_ref[0, 0, 2:3, :] * _dot(y, wout_ref[...])
    o_ref[0] = _rms(x1, fg_ref[...])


def _odd_call(x, ada4, layer, norm_g, w_gla, w_sb, w_rank, gate_w, gate_b, gla_norm_g, out_w, final_g):
    b, sq, d = x.shape
    ts = TS_ODD
    return pl.pallas_call(
        functools.partial(_odd_kernel, ts=ts),
        grid=(b, sq // ts),
        in_specs=[
            pl.BlockSpec((1, ts, d), lambda i, j: (i, j, 0)),
            pl.BlockSpec((1, 1, 3, d), lambda i, j: (layer, i, 0, 0)),
            _resident(norm_g.shape),
            _resident(w_gla.shape),
            _resident(w_sb.shape),
            _resident(w_rank.shape),
            _resident(gate_w.shape),
            _resident(gate_b.shape),
            _resident(gla_norm_g.shape),
            _resident(out_w.shape),
            _resident(final_g.shape),
        ],
        out_specs=pl.BlockSpec((1, ts, d), lambda i, j: (i, j, 0)),
        out_shape=jax.ShapeDtypeStruct(x.shape, F32),
        scratch_shapes=[
            pltpu.VMEM((GLA_HEADS, GLA_DK, GLA_DV), F32),
            pltpu.VMEM((SB_HEADS, sq, SB_DH), BF16),
            pltpu.VMEM((SB_HEADS, sq // ts, SB_DH, ts), BF16),
            pltpu.VMEM((SB_HEADS, SB_DH, ts), BF16),
            pltpu.VMEM((SB_HEADS, SB_DH, ts), F32),
            pltpu.VMEM((SB_HEADS, 1, ts), F32),
            pltpu.VMEM((ts, 2 * d), F32),
            pltpu.VMEM((ts, 2 * d), F32),
        ],
        compiler_params=pltpu.CompilerParams(
            dimension_semantics=("arbitrary", "arbitrary"), vmem_limit_bytes=VMEM_LIMIT),
        name="odd_layer",
    )(x, ada4, norm_g, w_gla, w_sb, w_rank, gate_w, gate_b, gla_norm_g, out_w, final_g)


def kernel(x, c, ada_w, ada_b, norm_g, even_in_w, pool_w, pool_scale, sgu_norm_g, sgu_w, sgu_b, even_out_w,
           odd_in_w, gla_gate_w, gla_gate_b, gla_norm_g, odd_out_w, final_g):
    depth, d, _ = ada_w.shape
    assert depth == 2 and even_in_w.shape[0] == 1 and odd_in_w.shape[0] == 1
    b = x.shape[0]
    ada4 = _ada_call(c, ada_w, ada_b).reshape(depth, b, 3, d)

    x = _even_call(
        x, ada4, 0, norm_g[0:1], even_in_w[0].astype(BF16), pool_w[0].astype(BF16), pool_scale[0:1],
        sgu_norm_g[0:1], sgu_w[0], sgu_b[0].T, even_out_w[0].astype(BF16))

    w = odd_in_w[0]
    glr0 = 2 * GLA_HEADS * GLA_DK + GLA_HEADS * GLA_DV
    w_gla = w[:, :glr0].astype(BF16)
    w_sb = w[:, glr0 + GLA_RANK:].astype(BF16)
    w_rank = jnp.pad(w[:, glr0:glr0 + GLA_RANK], ((0, 0), (0, GLA_RANK_PAD - GLA_RANK))).astype(BF16)
    gate_w = jnp.pad(gla_gate_w[0], ((0, GLA_RANK_PAD - GLA_RANK), (0, 0)))
    return _odd_call(
        x, ada4, 1, norm_g[1:2], w_gla, w_sb, w_rank, gate_w, gla_gate_b[0:1], gla_norm_g[0].reshape(1, -1),
        odd_out_w[0].astype(BF16), final_g.reshape(1, -1))
```

```python
import functools

import jax
import jax.numpy as jnp
from jax import lax
from jax.experimental import pallas as pl
from jax.experimental.pallas import tpu as pltpu

F32 = jnp.float32
BF16 = jnp.bfloat16

EPS = 1e-6
LOG2E = 1.4426950408889634
POOL_WINDOWS = (2, 4, 8, 16)
POOL_GROUP = 256
POOL_HALO = 32
SGU_LEN = 128
SGU_HEADS = 8
GLA_HEADS = 4
GLA_DK = 128
GLA_DV = 256
GLA_CHUNK = 64
GLA_RANK = 16
GLA_TAU = 16.0
SB_HEADS = 8
SB_DH = 128
SB_GROUP = 4
SB_STAGES = 4
SB_DEAD_CARRY = 105.0
SB_NO_BLOCK = 1.0e4

TS_EVEN = 512
TS_ODD = 256
VMEM_LIMIT = 56 * 1024 * 1024


def _dot(a, b):
    return jnp.dot(a, b, preferred_element_type=F32)


def _dot_nt(a, b):
    return lax.dot_general(a, b, (((1,), (1,)), ((), ())), preferred_element_type=F32)


def _split2(x):
    hi = x.astype(BF16)
    lo = (x - hi.astype(F32)).astype(BF16)
    return hi, lo


def _sigmoid(x):
    return 1.0 / (1.0 + jnp.exp(-x))


def _silu(x):
    return x * _sigmoid(x)


def _softplus(x):
    return jnp.maximum(x, 0.0) + jnp.log(1.0 + jnp.exp(-jnp.abs(x)))


def _rms(x, g):
    return x * lax.rsqrt(jnp.mean(x * x, axis=-1, keepdims=True) + EPS) * g


def _resident(shape):
    n = len(shape)
    return pl.BlockSpec(shape, lambda *_: (0,) * n, pipeline_mode=pl.Buffered(1))


def _ada_kernel(c_ref, w_ref, b_ref, o_ref):
    sc = _silu(c_ref[...])
    o_ref[0] = jnp.dot(sc, w_ref[0], preferred_element_type=F32,
                       precision=lax.Precision.HIGHEST) + b_ref[0]


def _ada_call(c, ada_w, ada_b):
    depth, d, d3 = ada_w.shape
    b = c.shape[0]
    tn = 1024
    return pl.pallas_call(
        _ada_kernel,
        grid=(depth, d3 // tn),
        in_specs=[
            pl.BlockSpec((b, d), lambda i, j: (0, 0)),
            pl.BlockSpec((1, d, tn), lambda i, j: (i, 0, j)),
            pl.BlockSpec((1, 1, tn), lambda i, j: (i, 0, j)),
        ],
        out_specs=pl.BlockSpec((1, b, tn), lambda i, j: (i, 0, j)),
        out_shape=jax.ShapeDtypeStruct((depth, b, d3), F32),
        compiler_params=pltpu.CompilerParams(
            dimension_semantics=("arbitrary", "arbitrary"), vmem_limit_bytes=VMEM_LIMIT),
        name="ada",
    )(c, ada_w, ada_b.reshape(depth, 1, d3))


def _even_kernel(x_ref, ada_ref, ng_ref, win_ref, pw_ref, ps_ref, sg_ref, sw_ref, sbt_ref, wout_ref,
                 o_ref, abuf, s2buf, s4buf, s8buf, s16buf, *, ts):
    s = pl.program_id(1)
    d = x_ref.shape[-1]
    hl = POOL_HALO
    g = POOL_GROUP

    @pl.when(s == 0)
    def _():
        abuf[0:hl, :] = jnp.zeros((hl, d), F32)

    @pl.when(s > 0)
    def _():
        abuf[0:hl, :] = abuf[ts:ts + hl, :]

    shift = ada_ref[0, 0, 0:1, :]
    scale = ada_ref[0, 0, 1:2, :]
    h = (_rms(x_ref[0], ng_ref[...]) * (1.0 + scale) + shift).astype(BF16)

    a = _dot(h, win_ref[:, 0:d])
    abuf[hl:hl + ts, :] = a
    u = _dot(h, win_ref[:, d:2 * d])

    n2 = ts + hl - 8
    s2buf[8:8 + n2, :] = abuf[8:8 + n2, :] + abuf[7:7 + n2, :]
    n4 = ts + hl - 16
    s4buf[16:16 + n4, :] = s2buf[16:16 + n4, g:] + s2buf[14:14 + n4, g:]
    n8 = ts + hl - 24
    s8buf[24:24 + n8, :] = s4buf[24:24 + n8, g:] + s4buf[20:20 + n8, g:]
    s16buf[hl:hl + ts, :] = s8buf[hl:hl + ts, g:] + s8buf[hl - 8:hl - 8 + ts, g:]
    v = _dot(h, win_ref[:, 2 * d:3 * d])

    t_glob = s * ts + lax.broadcasted_iota(jnp.int32, (ts, 1), 0)
    win_sums = (s2buf[hl:hl + ts, 0:g], s4buf[hl:hl + ts, 0:g], s8buf[hl:hl + ts, 0:g], s16buf[hl:hl + ts, :])
    ya = []
    for gi, w in enumerate(POOL_WINDOWS):
        count = jnp.minimum(t_glob + 1, w).astype(F32)
        p = win_sums[gi] / count - a[:, gi * g:(gi + 1) * g]
        ya.append(_dot(p.astype(BF16), pw_ref[gi]))
    gate_a = _silu(_dot(h, win_ref[:, 3 * d:4 * d]))
    ya = jnp.concatenate(ya, axis=1) * ps_ref[...] * gate_a

    vn = _rms(v, sg_ref[...]).astype(BF16)
    gate_b = _silu(_dot(h, win_ref[:, 4 * d:5 * d]))
    nblk = ts // SGU_LEN
    row = lax.broadcasted_iota(jnp.int32, (SGU_LEN, SGU_LEN), 0)
    col = lax.broadcasted_iota(jnp.int32, (SGU_LEN, SGU_LEN), 1)
    hd = d // SGU_HEADS
    z_cols = []
    for hh in range(SGU_HEADS):
        wm = jnp.where(col <= row, sw_ref[hh], 0.0).astype(BF16)
        vh = jnp.concatenate(
            [vn[n * SGU_LEN:(n + 1) * SGU_LEN, hh * hd:(hh + 1) * hd] for n in range(nblk)], axis=1)
        zh = _dot(wm, vh) + sbt_ref[:, hh:hh + 1]
        z_cols.append(jnp.concatenate([zh[:, n * hd:(n + 1) * hd] for n in range(nblk)], axis=0))
    yb = u * jnp.concatenate(z_cols, axis=1) * gate_b

    y = jnp.concatenate([ya, yb], axis=1).astype(BF16)
    o_ref[0] = x_ref[0] + ada_ref[0, 0, 2:3, :] * _dot(y, wout_ref[...])


def _even_call(x, ada4, layer, norm_g, in_w, pool_w, pool_scale, sgu_norm_g, sgu_w, sgu_bt, out_w):
    b, sq, d = x.shape
    ts = TS_EVEN
    g = POOL_GROUP
    rows = ts + POOL_HALO
    return pl.pallas_call(
        functools.partial(_even_kernel, ts=ts),
        grid=(b, sq // ts),
        in_specs=[
            pl.BlockSpec((1, ts, d), lambda i, j: (i, j, 0)),
            pl.BlockSpec((1, 1, 3, d), lambda i, j: (layer, i, 0, 0)),
            _resident(norm_g.shape),
            _resident(in_w.shape),
            _resident(pool_w.shape),
            _resident(pool_scale.shape),
            _resident(sgu_norm_g.shape),
            _resident(sgu_w.shape),
            _resident(sgu_bt.shape),
            _resident(out_w.shape),
        ],
        out_specs=pl.BlockSpec((1, ts, d), lambda i, j: (i, j, 0)),
        out_shape=jax.ShapeDtypeStruct(x.shape, F32),
        scratch_shapes=[
            pltpu.VMEM((rows, d), F32),
            pltpu.VMEM((rows, d), F32),
            pltpu.VMEM((rows, d - g), F32),
            pltpu.VMEM((rows, d - 2 * g), F32),
            pltpu.VMEM((rows, d - 3 * g), F32),
        ],
        compiler_params=pltpu.CompilerParams(
            dimension_semantics=("arbitrary", "arbitrary"), vmem_limit_bytes=VMEM_LIMIT),
        name="even_layer",
    )(x, ada4, norm_g, in_w, pool_w, pool_scale, sgu_norm_g, sgu_w, sgu_bt, out_w)


def _odd_kernel(x_ref, ada_ref, ng_ref, wt_ref, gw_ref, gb_ref, gng_ref, wout_ref, fg_ref,
                o_ref, state, kbuf, vtbuf, qtbuf, acc, crow, ybuf, gbuf, *, ts):
    s = pl.program_id(1)

    @pl.when(s == 0)
    def _():
        state[...] = jnp.zeros(state.shape, F32)

    shift = ada_ref[0, 0, 0:1, :]
    scale = ada_ref[0, 0, 1:2, :]
    h = (_rms(x_ref[0], ng_ref[...]) * (1.0 + scale) + shift).astype(BF16)

    kw = GLA_HEADS * GLA_DK
    vw = GLA_HEADS * GLA_DV
    sbw = SB_HEADS * SB_DH
    r_gq, r_gk, r_gv = 0, kw, 2 * kw
    r_glr = r_gv + vw
    r_sq = r_glr + GLA_RANK
    r_sk, r_sv, r_gate = r_sq + sbw, r_sq + 2 * sbw, r_sq + 3 * sbw

    def proj(r0, n):
        return _dot_nt(h, wt_ref[r0:r0 + n, :])

    sb_scale = SB_DH ** -0.5
    s0 = pl.multiple_of(s * ts, ts)

    glr = proj(r_glr, GLA_RANK)
    gq = proj(r_gq, kw) * (GLA_DK ** -0.5)
    glr_hi, glr_lo = _split2(glr)
    gw_hi, gw_lo = _split2(gw_ref[...])
    pre = _dot(jnp.concatenate([glr_hi, glr_lo, glr_hi], axis=1),
               jnp.concatenate([gw_hi, gw_hi, gw_lo], axis=0)) + gb_ref[...]
    gk = proj(r_gk, kw)
    la = -_softplus(-pre) * (1.0 / GLA_TAU)
    gv_b = proj(r_gv, vw).astype(BF16)
    r = lax.broadcasted_iota(jnp.int32, (ts, ts), 0)
    cc = lax.broadcasted_iota(jnp.int32, (ts, ts), 1)
    same_chunk_causal = (cc <= r) & (cc // GLA_CHUNK == r // GLA_CHUNK)
    tri = jnp.where(same_chunk_causal, 1.0, 0.0).astype(BF16)
    la_hi, la_lo = _split2(la)
    sq = proj(r_sq, sbw) * sb_scale
    bcum = _dot(jnp.concatenate([tri, tri], axis=1), jnp.concatenate([la_hi, la_lo], axis=0))
    sk = proj(r_sk, sbw).astype(BF16)
    sv = proj(r_sv, sbw)
    for hh in range(SB_HEADS):
        hs = slice(hh * SB_DH, (hh + 1) * SB_DH)
        qtbuf[hh] = sq[:, hs].T.astype(BF16)
        kbuf[hh, pl.ds(s0, ts), :] = sk[:, hs]
        vtbuf[hh, s] = sv[:, hs].T.astype(BF16)

    nchunk = ts // GLA_CHUNK
    q_dec = (gq * jnp.exp(bcum)).astype(BF16)
    k_inv = (gk * jnp.exp(-bcum)).astype(BF16)
    b_last = jnp.concatenate(
        [jnp.broadcast_to(bcum[(ci + 1) * GLA_CHUNK - 1:(ci + 1) * GLA_CHUNK, :], (GLA_CHUNK, kw))
         for ci in range(nchunk)], axis=0)
    k_end = gk * jnp.exp(b_last - bcum)

    heads = range(GLA_HEADS)
    kss = [slice(hh * GLA_DK, (hh + 1) * GLA_DK) for hh in heads]
    vss = [slice(hh * GLA_DV, (hh + 1) * GLA_DV) for hh in heads]
    zs = [_dot_nt(q_dec[:, kss[hh]], k_inv[:, kss[hh]]) for hh in heads]
    lane_chunk = lax.broadcasted_iota(jnp.int32, (GLA_DK, ts), 1) // GLA_CHUNK
    us, bcts = [], []
    for hh in heads:
        ket = k_end[:, kss[hh]].T
        bcts.append(bcum[:, kss[hh]].T)
        lhs = jnp.concatenate(
            [jnp.where(lane_chunk == ci, ket, 0.0).astype(BF16) for ci in range(nchunk)], axis=0)
        us.append(_dot(lhs, gv_b[:, vss[hh]]))
    zero_blk = jnp.zeros((GLA_CHUNK, GLA_DK), BF16)
    for hh in heads:
        att = jnp.where(same_chunk_causal, zs[hh], 0.0).astype(BF16)
        st = state[hh]
        starts = []
        for ci in range(nchunk):
            starts.append(st.astype(BF16))
            last = (ci + 1) * GLA_CHUNK - 1
            st = jnp.exp(bcts[hh][:, last:last + 1]) * st + us[hh][ci * GLA_DK:(ci + 1) * GLA_DK, :]
        state[hh] = st
        qh = q_dec[:, kss[hh]]
        q_blocks = jnp.concatenate(
            [jnp.concatenate([qh[ci * GLA_CHUNK:(ci + 1) * GLA_CHUNK, :] if cj == ci else zero_blk
                              for cj in range(nchunk)], axis=1) for ci in range(nchunk)], axis=0)
        o = _dot(jnp.concatenate([att, q_blocks], axis=1),
                 jnp.concatenate([gv_b[:, vss[hh]]] + starts, axis=0))
        ybuf[:, vss[hh]] = _rms(o, gng_ref[:, vss[hh]])

    groups = [tuple(range(g, g + SB_GROUP)) for g in range(0, SB_HEADS, SB_GROUP)]
    gate_chunk = (vw + sbw) // (SB_STAGES * len(groups))

    def gate_filler(ci):
        def run():
            lo = ci * gate_chunk
            gbuf[:, lo:lo + gate_chunk] = _silu(proj(r_gate + lo, gate_chunk))
        return run

    fillers = [gate_filler(ci) for ci in range(SB_STAGES * len(groups))]

    jr = lax.broadcasted_iota(jnp.int32, (ts, ts), 0)
    jc = lax.broadcasted_iota(jnp.int32, (ts, ts), 1)
    later = jnp.where(jc > jr, 1.0, 0.0).astype(BF16)
    valid = jr < jc

    def sb_blocks(heads, kbs, fresh, penalty=None, between=()):
        between = list(between)

        def stage_done():
            if between:
                between.pop(0)()

        zs = [[_dot(kbuf[hh, pl.ds(pl.multiple_of(kb * ts, ts), ts), :], qtbuf[hh]) for hh in heads] for kb in kbs]
        stage_done()
        sps, lbetas, firsts = [], [], []
        for bi, zb in enumerate(zs):
            sps.append([])
            lbetas.append([])
            firsts.append([])
            for z in zb:
                lg = jnp.log(1.0 + jnp.exp2(jnp.abs(z) * (-LOG2E)))
                sp = jnp.maximum(z, 0.0) + lg
                if fresh and bi == 0:
                    sp = jnp.where(valid, sp, 0.0)
                sps[bi].append(sp.astype(BF16))
                firsts[bi].append(sp[0:1, :])
                lbetas[bi].append(jnp.minimum(z, 0.0) - lg)
        stage_done()
        sums = [[_dot(later, sp) for sp in spb] for spb in sps]
        stage_done()
        ws, carries = [], []
        for hi, hh in enumerate(heads):
            carry = None if fresh else crow[hh]
            ws.append([])
            for bi in range(len(kbs)):
                xarg = lbetas[bi][hi] - sums[bi][hi]
                if carry is not None:
                    xarg = xarg - carry
                w = jnp.exp(xarg)
                if fresh and bi == 0:
                    w = jnp.where(valid, w, 0.0)
                ws[hi].append(w.astype(BF16))
                total = sums[bi][hi][0:1, :] + firsts[bi][hi]
                carry = total if carry is None else carry + total
                if penalty is not None and bi == 0:
                    carry = carry + penalty
            carries.append(carry)
        stage_done()
        for hi, hh in enumerate(heads):
            contrib = _dot(jnp.concatenate([vtbuf[hh, kb] for kb in kbs], axis=1),
                           jnp.concatenate(ws[hi], axis=0))
            acc[hh] = contrib if fresh else acc[hh] + contrib
            crow[hh] = carries[hi]

    def any_alive():
        low = crow[0]
        for hh in range(1, SB_HEADS):
            low = jnp.minimum(low, crow[hh])
        return (jnp.min(low) < SB_DEAD_CARRY).astype(jnp.int32)

    penalty = jnp.where(s > 0, 0.0, SB_NO_BLOCK)
    for gi, heads in enumerate(groups):
        sb_blocks(heads, (s, jnp.maximum(s - 1, 0)), True, penalty, fillers[gi * SB_STAGES:(gi + 1) * SB_STAGES])

    def more_blocks(state_):
        i, alive = state_
        return jnp.logical_and(i < s, alive > 0)

    def key_block(state_):
        i, _ = state_
        for heads in groups:
            sb_blocks(heads, (s - 1 - i,), False)
        return i + 1, any_alive()

    lax.while_loop(more_blocks, key_block, (jnp.int32(1), any_alive()))

    for hh in range(SB_HEADS):
        ybuf[:, vw + hh * SB_DH:vw + (hh + 1) * SB_DH] = acc[hh].T

    y = (ybuf[...] * gbuf[...]).astype(BF16)
    x1 = x_ref[0] + ada_ref[0, 0, 2:3, :] * _dot(y, wout_ref[...])
    o_ref[0] = _rms(x1, fg_ref[...])


def _odd_call(x, ada4, layer, norm_g, in_wt, gate_w, gate_b, gla_norm_g, out_w, final_g):
    b, sq, d = x.shape
    ts = TS_ODD
    return pl.pallas_call(
        functools.partial(_odd_kernel, ts=ts),
        grid=(b, sq // ts),
        in_specs=[
            pl.BlockSpec((1, ts, d), lambda i, j: (i, j, 0)),
            pl.BlockSpec((1, 1, 3, d), lambda i, j: (layer, i, 0, 0)),
            _resident(norm_g.shape),
            _resident(in_wt.shape),
            _resident(gate_w.shape),
            _resident(gate_b.shape),
            _resident(gla_norm_g.shape),
            _resident(out_w.shape),
            _resident(final_g.shape),
        ],
        out_specs=pl.BlockSpec((1, ts, d), lambda i, j: (i, j, 0)),
        out_shape=jax.ShapeDtypeStruct(x.shape, F32),
        scratch_shapes=[
            pltpu.VMEM((GLA_HEADS, GLA_DK, GLA_DV), F32),
            pltpu.VMEM((SB_HEADS, sq, SB_DH), BF16),
            pltpu.VMEM((SB_HEADS, sq // ts, SB_DH, ts), BF16),
            pltpu.VMEM((SB_HEADS, SB_DH, ts), BF16),
            pltpu.VMEM((SB_HEADS, SB_DH, ts), F32),
            pltpu.VMEM((SB_HEADS, 1, ts), F32),
            pltpu.VMEM((ts, 2 * d), F32),
            pltpu.VMEM((ts, 2 * d), F32),
        ],
        compiler_params=pltpu.CompilerParams(
            dimension_semantics=("arbitrary", "arbitrary"), vmem_limit_bytes=VMEM_LIMIT),
        name="odd_layer",
    )(x, ada4, norm_g, in_wt, gate_w, gate_b, gla_norm_g, out_w, final_g)


def kernel(x, c, ada_w, ada_b, norm_g, even_in_w, pool_w, pool_scale, sgu_norm_g, sgu_w, sgu_b, even_out_w,
           odd_in_w, gla_gate_w, gla_gate_b, gla_norm_g, odd_out_w, final_g):
    depth, d, _ = ada_w.shape
    assert depth == 2 and even_in_w.shape[0] == 1 and odd_in_w.shape[0] == 1
    b = x.shape[0]
    ada4 = _ada_call(c, ada_w, ada_b).reshape(depth, b, 3, d)

    x = _even_call(
        x, ada4, 0, norm_g[0:1], even_in_w[0].astype(BF16), pool_w[0].astype(BF16), pool_scale[0:1],
        sgu_norm_g[0:1], sgu_w[0], sgu_b[0].T, even_out_w[0].astype(BF16))

    in_wt = jnp.swapaxes(odd_in_w[0], 0, 1).astype(BF16)
    return _odd_call(
        x, ada4, 1, norm_g[1:2], in_wt, gla_gate_w[0], gla_gate_b[0:1], gla_norm_g[0].reshape(1, -1),
        odd_out_w[0].astype(BF16), final_g.reshape(1, -1))
```

```python
import functools

import jax
import jax.numpy as jnp
from jax import lax
from jax.experimental import pallas as pl
from jax.experimental.pallas import tpu as pltpu

F32 = jnp.float32
BF16 = jnp.bfloat16

EPS = 1e-6
LOG2E = 1.4426950408889634
POOL_WINDOWS = (2, 4, 8, 16)
POOL_GROUP = 256
POOL_HALO = 32
SGU_LEN = 128
SGU_HEADS = 8
GLA_HEADS = 4
GLA_DK = 128
GLA_DV = 256
GLA_CHUNK = 64
GLA_RANK = 16
GLA_TAU = 16.0
SB_HEADS = 8
SB_DH = 128
SB_GROUP = 4
SB_STAGES = 4
SB_DEAD_CARRY = 105.0
SB_NO_BLOCK = 1.0e4

TS_EVEN = 512
TS_ODD = 256
VMEM_LIMIT = 56 * 1024 * 1024


def _dot(a, b):
    return jnp.dot(a, b, preferred_element_type=F32)


def _dot_nt(a, b):
    return lax.dot_general(a, b, (((1,), (1,)), ((), ())), preferred_element_type=F32)


def _split2(x):
    hi = x.astype(BF16)
    lo = (x - hi.astype(F32)).astype(BF16)
    return hi, lo


def _sigmoid(x):
    return 1.0 / (1.0 + jnp.exp(-x))


def _silu(x):
    return x * _sigmoid(x)


def _softplus(x):
    return jnp.maximum(x, 0.0) + jnp.log(1.0 + jnp.exp(-jnp.abs(x)))


def _rms(x, g):
    return x * lax.rsqrt(jnp.mean(x * x, axis=-1, keepdims=True) + EPS) * g


def _resident(shape):
    n = len(shape)
    return pl.BlockSpec(shape, lambda *_: (0,) * n, pipeline_mode=pl.Buffered(1))


def _ada_kernel(c_ref, w_ref, b_ref, o_ref):
    sc = _silu(c_ref[...])
    o_ref[0] = jnp.dot(sc, w_ref[0], preferred_element_type=F32,
                       precision=lax.Precision.HIGHEST) + b_ref[0]


def _ada_call(c, ada_w, ada_b):
    depth, d, d3 = ada_w.shape
    b = c.shape[0]
    tn = 1024
    return pl.pallas_call(
        _ada_kernel,
        grid=(depth, d3 // tn),
        in_specs=[
            pl.BlockSpec((b, d), lambda i, j: (0, 0)),
            pl.BlockSpec((1, d, tn), lambda i, j: (i, 0, j)),
            pl.BlockSpec((1, 1, tn), lambda i, j: (i, 0, j)),
        ],
        out_specs=pl.BlockSpec((1, b, tn), lambda i, j: (i, 0, j)),
        out_shape=jax.ShapeDtypeStruct((depth, b, d3), F32),
        compiler_params=pltpu.CompilerParams(
            dimension_semantics=("arbitrary", "arbitrary"), vmem_limit_bytes=VMEM_LIMIT),
        name="ada",
    )(c, ada_w, ada_b.reshape(depth, 1, d3))


def _even_kernel(x_ref, ada_ref, ng_ref, win_ref, pw_ref, ps_ref, sg_ref, sw_ref, sbt_ref, wout_ref,
                 o_ref, abuf, s2buf, s4buf, s8buf, s16buf, *, ts):
    s = pl.program_id(1)
    d = x_ref.shape[-1]
    hl = POOL_HALO
    g = POOL_GROUP

    @pl.when(s == 0)
    def _():
        abuf[0:hl, :] = jnp.zeros((hl, d), F32)

    @pl.when(s > 0)
    def _():
        abuf[0:hl, :] = abuf[ts:ts + hl, :]

    shift = ada_ref[0, 0, 0:1, :]
    scale = ada_ref[0, 0, 1:2, :]
    h = (_rms(x_ref[0], ng_ref[...] * (1.0 + scale)) + shift).astype(BF16)

    a = _dot(h, win_ref[:, 0:d])
    abuf[hl:hl + ts, :] = a
    u = _dot(h, win_ref[:, d:2 * d])

    n2 = ts + hl - 8
    s2buf[8:8 + n2, :] = abuf[8:8 + n2, :] + abuf[7:7 + n2, :]
    n4 = ts + hl - 16
    s4buf[16:16 + n4, :] = s2buf[16:16 + n4, g:] + s2buf[14:14 + n4, g:]
    n8 = ts + hl - 24
    s8buf[24:24 + n8, :] = s4buf[24:24 + n8, g:] + s4buf[20:20 + n8, g:]
    s16buf[hl:hl + ts, :] = s8buf[hl:hl + ts, g:] + s8buf[hl - 8:hl - 8 + ts, g:]
    v = _dot(h, win_ref[:, 2 * d:3 * d])

    t_glob = s * ts + lax.broadcasted_iota(jnp.int32, (ts, 1), 0)
    win_sums = (s2buf[hl:hl + ts, 0:g], s4buf[hl:hl + ts, 0:g], s8buf[hl:hl + ts, 0:g], s16buf[hl:hl + ts, :])
    ya = []
    for gi, w in enumerate(POOL_WINDOWS):
        count = jnp.minimum(t_glob + 1, w).astype(F32)
        p = win_sums[gi] / count - a[:, gi * g:(gi + 1) * g]
        ya.append(_dot(p.astype(BF16), pw_ref[gi]))
    gate_a = _silu(_dot(h, win_ref[:, 3 * d:4 * d]))
    ya = jnp.concatenate(ya, axis=1) * ps_ref[...] * gate_a

    vn = _rms(v, sg_ref[...]).astype(BF16)
    gate_b = _silu(_dot(h, win_ref[:, 4 * d:5 * d]))
    nblk = ts // SGU_LEN
    row = lax.broadcasted_iota(jnp.int32, (SGU_LEN, SGU_LEN), 0)
    col = lax.broadcasted_iota(jnp.int32, (SGU_LEN, SGU_LEN), 1)
    hd = d // SGU_HEADS
    z_cols = []
    for hh in range(SGU_HEADS):
        wm = jnp.where(col <= row, sw_ref[hh], 0.0).astype(BF16)
        vh = jnp.concatenate(
            [vn[n * SGU_LEN:(n + 1) * SGU_LEN, hh * hd:(hh + 1) * hd] for n in range(nblk)], axis=1)
        zh = _dot(wm, vh) + sbt_ref[:, hh:hh + 1]
        z_cols.append(jnp.concatenate([zh[:, n * hd:(n + 1) * hd] for n in range(nblk)], axis=0))
    yb = u * jnp.concatenate(z_cols, axis=1) * gate_b

    y = jnp.concatenate([ya, yb], axis=1).astype(BF16)
    o_ref[0] = x_ref[0] + ada_ref[0, 0, 2:3, :] * _dot(y, wout_ref[...])


def _even_call(x, ada4, layer, norm_g, in_w, pool_w, pool_scale, sgu_norm_g, sgu_w, sgu_bt, out_w):
    b, sq, d = x.shape
    ts = TS_EVEN
    g = POOL_GROUP
    rows = ts + POOL_HALO
    return pl.pallas_call(
        functools.partial(_even_kernel, ts=ts),
        grid=(b, sq // ts),
        in_specs=[
            pl.BlockSpec((1, ts, d), lambda i, j: (i, j, 0)),
            pl.BlockSpec((1, 1, 3, d), lambda i, j: (layer, i, 0, 0)),
            _resident(norm_g.shape),
            _resident(in_w.shape),
            _resident(pool_w.shape),
            _resident(pool_scale.shape),
            _resident(sgu_norm_g.shape),
            _resident(sgu_w.shape),
            _resident(sgu_bt.shape),
            _resident(out_w.shape),
        ],
        out_specs=pl.BlockSpec((1, ts, d), lambda i, j: (i, j, 0)),
        out_shape=jax.ShapeDtypeStruct(x.shape, F32),
        scratch_shapes=[
            pltpu.VMEM((rows, d), F32),
            pltpu.VMEM((rows, d), F32),
            pltpu.VMEM((rows, d - g), F32),
            pltpu.VMEM((rows, d - 2 * g), F32),
            pltpu.VMEM((rows, d - 3 * g), F32),
        ],
        compiler_params=pltpu.CompilerParams(
            dimension_semantics=("arbitrary", "arbitrary"), vmem_limit_bytes=VMEM_LIMIT),
        name="even_layer",
    )(x, ada4, norm_g, in_w, pool_w, pool_scale, sgu_norm_g, sgu_w, sgu_bt, out_w)


def _odd_kernel(x_ref, ada_ref, ng_ref, wt_ref, gw_ref, gb_ref, gng_ref, wout_ref, fg_ref,
                o_ref, state, kbuf, vtbuf, qtbuf, acc, crow, ybuf, gbuf, *, ts):
    s = pl.program_id(1)

    @pl.when(s == 0)
    def _():
        state[...] = jnp.zeros(state.shape, F32)

    shift = ada_ref[0, 0, 0:1, :]
    scale = ada_ref[0, 0, 1:2, :]
    h = (_rms(x_ref[0], ng_ref[...] * (1.0 + scale)) + shift).astype(BF16)

    kw = GLA_HEADS * GLA_DK
    vw = GLA_HEADS * GLA_DV
    sbw = SB_HEADS * SB_DH
    r_gq, r_gk, r_gv = 0, kw, 2 * kw
    r_glr = r_gv + vw
    r_sq = r_glr + GLA_RANK
    r_sk, r_sv, r_gate = r_sq + sbw, r_sq + 2 * sbw, r_sq + 3 * sbw

    def proj(r0, n):
        return _dot_nt(h, wt_ref[r0:r0 + n, :])

    sb_scale = SB_DH ** -0.5
    s0 = pl.multiple_of(s * ts, ts)

    glr = proj(r_glr, GLA_RANK)
    gq = proj(r_gq, kw) * (GLA_DK ** -0.5)
    glr_hi, glr_lo = _split2(glr)
    gw_hi, gw_lo = _split2(gw_ref[...])
    pre = _dot(jnp.concatenate([glr_hi, glr_lo, glr_hi], axis=1),
               jnp.concatenate([gw_hi, gw_hi, gw_lo], axis=0)) + gb_ref[...]
    gk = proj(r_gk, kw)
    la = -_softplus(-pre) * (1.0 / GLA_TAU)
    gv_b = proj(r_gv, vw).astype(BF16)
    r = lax.broadcasted_iota(jnp.int32, (ts, ts), 0)
    cc = lax.broadcasted_iota(jnp.int32, (ts, ts), 1)
    same_chunk_causal = (cc <= r) & (cc // GLA_CHUNK == r // GLA_CHUNK)
    tri = jnp.where(same_chunk_causal, 1.0, 0.0).astype(BF16)
    la_hi, la_lo = _split2(la)
    sq = proj(r_sq, sbw) * sb_scale
    bcum = _dot(jnp.concatenate([tri, tri], axis=1), jnp.concatenate([la_hi, la_lo], axis=0))
    sk = proj(r_sk, sbw).astype(BF16)
    sv = proj(r_sv, sbw)
    for hh in range(SB_HEADS):
        hs = slice(hh * SB_DH, (hh + 1) * SB_DH)
        qtbuf[hh] = sq[:, hs].T.astype(BF16)
        kbuf[hh, pl.ds(s0, ts), :] = sk[:, hs]
        vtbuf[hh, s] = sv[:, hs].T.astype(BF16)

    nchunk = ts // GLA_CHUNK
    q_dec = (gq * jnp.exp(bcum)).astype(BF16)
    k_inv = (gk * jnp.exp(-bcum)).astype(BF16)
    b_last = jnp.concatenate(
        [jnp.broadcast_to(bcum[(ci + 1) * GLA_CHUNK - 1:(ci + 1) * GLA_CHUNK, :], (GLA_CHUNK, kw))
         for ci in range(nchunk)], axis=0)
    k_end = gk * jnp.exp(b_last - bcum)

    heads = range(GLA_HEADS)
    kss = [slice(hh * GLA_DK, (hh + 1) * GLA_DK) for hh in heads]
    vss = [slice(hh * GLA_DV, (hh + 1) * GLA_DV) for hh in heads]
    zs = [_dot_nt(q_dec[:, kss[hh]], k_inv[:, kss[hh]]) for hh in heads]
    lane_chunk = lax.broadcasted_iota(jnp.int32, (GLA_DK, ts), 1) // GLA_CHUNK
    us, bcts = [], []
    for hh in heads:
        ket = k_end[:, kss[hh]].T
        bcts.append(bcum[:, kss[hh]].T)
        lhs = jnp.concatenate(
            [jnp.where(lane_chunk == ci, ket, 0.0).astype(BF16) for ci in range(nchunk)], axis=0)
        us.append(_dot(lhs, gv_b[:, vss[hh]]))
    zero_blk = jnp.zeros((GLA_CHUNK, GLA_DK), BF16)
    for hh in heads:
        att = jnp.where(same_chunk_causal, zs[hh], 0.0).astype(BF16)
        st = state[hh]
        starts = []
        for ci in range(nchunk):
            starts.append(st.astype(BF16))
            last = (ci + 1) * GLA_CHUNK - 1
            st = jnp.exp(bcts[hh][:, last:last + 1]) * st + us[hh][ci * GLA_DK:(ci + 1) * GLA_DK, :]
        state[hh] = st
        qh = q_dec[:, kss[hh]]
        q_blocks = jnp.concatenate(
            [jnp.concatenate([qh[ci * GLA_CHUNK:(ci + 1) * GLA_CHUNK, :] if cj == ci else zero_blk
                              for cj in range(nchunk)], axis=1) for ci in range(nchunk)], axis=0)
        o = _dot(jnp.concatenate([att, q_blocks], axis=1),
                 jnp.concatenate([gv_b[:, vss[hh]]] + starts, axis=0))
        ybuf[:, vss[hh]] = _rms(o, gng_ref[:, vss[hh]])

    groups = [tuple(range(g, g + SB_GROUP)) for g in range(0, SB_HEADS, SB_GROUP)]
    gate_chunk = (vw + sbw) // (SB_STAGES * len(groups))

    def gate_filler(ci):
        def run():
            lo = ci * gate_chunk
            gbuf[:, lo:lo + gate_chunk] = _silu(proj(r_gate + lo, gate_chunk))
        return run

    fillers = [gate_filler(ci) for ci in range(SB_STAGES * len(groups))]

    jr = lax.broadcasted_iota(jnp.int32, (ts, ts), 0)
    jc = lax.broadcasted_iota(jnp.int32, (ts, ts), 1)
    later = jnp.where(jc > jr, 1.0, 0.0).astype(BF16)
    valid = jr < jc

    def sb_blocks(heads, kbs, fresh, penalty=None, between=()):
        between = list(between)

        def stage_done():
            if between:
                between.pop(0)()

        zs = [[_dot(kbuf[hh, pl.ds(pl.multiple_of(kb * ts, ts), ts), :], qtbuf[hh]) for hh in heads] for kb in kbs]
        stage_done()
        sps, lbetas, firsts = [], [], []
        for bi, zb in enumerate(zs):
            sps.append([])
            lbetas.append([])
            firsts.append([])
            for z in zb:
                lg = jnp.log(1.0 + jnp.exp2(jnp.abs(z) * (-LOG2E)))
                sp = jnp.maximum(z, 0.0) + lg
                if fresh and bi == 0:
                    sp = jnp.where(valid, sp, 0.0)
                sps[bi].append(sp.astype(BF16))
                firsts[bi].append(sp[0:1, :])
                lbetas[bi].append(jnp.minimum(z, 0.0) - lg)
        stage_done()
        sums = [[_dot(later, sp) for sp in spb] for spb in sps]
        stage_done()
        ws, carries = [], []
        for hi, hh in enumerate(heads):
            carry = None if fresh else crow[hh]
            ws.append([])
            for bi in range(len(kbs)):
                xarg = lbetas[bi][hi] - sums[bi][hi]
                if carry is not None:
                    xarg = xarg - carry
                w = jnp.exp(xarg)
                if fresh and bi == 0:
                    w = jnp.where(valid, w, 0.0)
                ws[hi].append(w.astype(BF16))
                total = sums[bi][hi][0:1, :] + firsts[bi][hi]
                carry = total if carry is None else carry + total
                if penalty is not None and bi == 0:
                    carry = carry + penalty
            carries.append(carry)
        stage_done()
        for hi, hh in enumerate(heads):
            contrib = _dot(jnp.concatenate([vtbuf[hh, kb] for kb in kbs], axis=1),
                           jnp.concatenate(ws[hi], axis=0))
            acc[hh] = contrib if fresh else acc[hh] + contrib
            crow[hh] = carries[hi]

    def any_alive():
        low = crow[0]
        for hh in range(1, SB_HEADS):
            low = jnp.minimum(low, crow[hh])
        return (jnp.min(low) < SB_DEAD_CARRY).astype(jnp.int32)

    penalty = jnp.where(s > 0, 0.0, SB_NO_BLOCK)
    for gi, heads in enumerate(groups):
        sb_blocks(heads, (s, jnp.maximum(s - 1, 0)), True, penalty, fillers[gi * SB_STAGES:(gi + 1) * SB_STAGES])

    def finish():
        for hh in range(SB_HEADS):
            ybuf[:, vw + hh * SB_DH:vw + (hh + 1) * SB_DH] = acc[hh].T
        y = (ybuf[...] * gbuf[...]).astype(BF16)
        x1 = x_ref[0] + ada_ref[0, 0, 2:3, :] * _dot(y, wout_ref[...])
        o_ref[0] = _rms(x1, fg_ref[...])

    finish()

    @pl.when(jnp.logical_and(s > 1, any_alive() > 0))
    def _():
        def more_blocks(state_):
            i, alive = state_
            return jnp.logical_and(i < s, alive > 0)

        def key_block(state_):
            i, _ = state_
            for heads in groups:
                sb_blocks(heads, (s - 1 - i,), False)
            return i + 1, any_alive()

        lax.while_loop(more_blocks, key_block, (jnp.int32(1), jnp.int32(1)))
        finish()


def _odd_call(x, ada4, layer, norm_g, in_wt, gate_w, gate_b, gla_norm_g, out_w, final_g):
    b, sq, d = x.shape
    ts = TS_ODD
    return pl.pallas_call(
        functools.partial(_odd_kernel, ts=ts),
        grid=(b, sq // ts),
        in_specs=[
            pl.BlockSpec((1, ts, d), lambda i, j: (i, j, 0)),
            pl.BlockSpec((1, 1, 3, d), lambda i, j: (layer, i, 0, 0)),
            _resident(norm_g.shape),
            _resident(in_wt.shape),
            _resident(gate_w.shape),
            _resident(gate_b.shape),
            _resident(gla_norm_g.shape),
            _resident(out_w.shape),
            _resident(final_g.shape),
        ],
        out_specs=pl.BlockSpec((1, ts, d), lambda i, j: (i, j, 0)),
        out_shape=jax.ShapeDtypeStruct(x.shape, F32),
        scratch_shapes=[
            pltpu.VMEM((GLA_HEADS, GLA_DK, GLA_DV), F32),
            pltpu.VMEM((SB_HEADS, sq, SB_DH), BF16),
            pltpu.VMEM((SB_HEADS, sq // ts, SB_DH, ts), BF16),
            pltpu.VMEM((SB_HEADS, SB_DH, ts), BF16),
            pltpu.VMEM((SB_HEADS, SB_DH, ts), F32),
            pltpu.VMEM((SB_HEADS, 1, ts), F32),
            pltpu.VMEM((ts, 2 * d), F32),
            pltpu.VMEM((ts, 2 * d), F32),
        ],
        compiler_params=pltpu.CompilerParams(
            dimension_semantics=("arbitrary", "arbitrary"), vmem_limit_bytes=VMEM_LIMIT),
        name="odd_layer",
    )(x, ada4, norm_g, in_wt, gate_w, gate_b, gla_norm_g, out_w, final_g)


def kernel(x, c, ada_w, ada_b, norm_g, even_in_w, pool_w, pool_scale, sgu_norm_g, sgu_w, sgu_b, even_out_w,
           odd_in_w, gla_gate_w, gla_gate_b, gla_norm_g, odd_out_w, final_g):
    depth, d, _ = ada_w.shape
    assert depth == 2 and even_in_w.shape[0] == 1 and odd_in_w.shape[0] == 1
    b = x.shape[0]
    ada4 = _ada_call(c, ada_w, ada_b).reshape(depth, b, 3, d)

    x = _even_call(
        x, ada4, 0, norm_g[0:1], even_in_w[0].astype(BF16), pool_w[0].astype(BF16), pool_scale[0:1],
        sgu_norm_g[0:1], sgu_w[0], sgu_b[0].T, even_out_w[0].astype(BF16))

    in_wt = jnp.swapaxes(odd_in_w[0], 0, 1).astype(BF16)
    return _odd_call(
        x, ada4, 1, norm_g[1:2], in_wt, gla_gate_w[0], gla_gate_b[0:1], gla_norm_g[0].reshape(1, -1),
        odd_out_w[0].astype(BF16), final_g.reshape(1, -1))
```

```python
import functools

import jax
import jax.numpy as jnp
from jax import lax
from jax.experimental import pallas as pl
from jax.experimental.pallas import tpu as pltpu

F32 = jnp.float32
BF16 = jnp.bfloat16

EPS = 1e-6
LOG2E = 1.4426950408889634
POOL_WINDOWS = (2, 4, 8, 16)
POOL_GROUP = 256
POOL_HALO = 32
SGU_LEN = 128
SGU_HEADS = 8
GLA_HEADS = 4
GLA_DK = 128
GLA_DV = 256
GLA_CHUNK = 64
GLA_RANK = 16
GLA_TAU = 16.0
SB_HEADS = 8
SB_DH = 128
SB_GROUP = 4
SB_STAGES = 4
SB_DEAD_CARRY = 105.0
SB_NO_BLOCK = 1.0e4

TS_EVEN = 512
TS_ODD = 256
VMEM_LIMIT = 56 * 1024 * 1024


def _dot(a, b):
    return jnp.dot(a, b, preferred_element_type=F32)


def _dot_nt(a, b):
    return lax.dot_general(a, b, (((1,), (1,)), ((), ())), preferred_element_type=F32)


def _split2(x):
    hi = x.astype(BF16)
    lo = (x - hi.astype(F32)).astype(BF16)
    return hi, lo


def _sigmoid(x):
    return 1.0 / (1.0 + jnp.exp(-x))


def _silu(x):
    return x * _sigmoid(x)


def _softplus(x):
    return jnp.maximum(x, 0.0) + jnp.log(1.0 + jnp.exp(-jnp.abs(x)))


def _rms(x, g):
    return x * lax.rsqrt(jnp.mean(x * x, axis=-1, keepdims=True) + EPS) * g


def _resident(shape):
    n = len(shape)
    return pl.BlockSpec(shape, lambda *_: (0,) * n, pipeline_mode=pl.Buffered(1))


def _rider_blocks(rows, steps):
    br = -(-rows // steps)
    br = -(-br // 16) * 16
    return br, -(-rows // br)


def _ada_kernel(c_ref, w_ref, b_ref, o_ref):
    sc = _silu(c_ref[...])
    o_ref[0] = jnp.dot(sc, w_ref[0], preferred_element_type=F32,
                       precision=lax.Precision.HIGHEST) + b_ref[0]


def _ada_call(c, ada_w, ada_b):
    depth, d, d3 = ada_w.shape
    b = c.shape[0]
    tn = 1024
    return pl.pallas_call(
        _ada_kernel,
        grid=(depth, d3 // tn),
        in_specs=[
            pl.BlockSpec((b, d), lambda i, j: (0, 0)),
            pl.BlockSpec((1, d, tn), lambda i, j: (i, 0, j)),
            pl.BlockSpec((1, 1, tn), lambda i, j: (i, 0, j)),
        ],
        out_specs=pl.BlockSpec((1, b, tn), lambda i, j: (i, 0, j)),
        out_shape=jax.ShapeDtypeStruct((depth, b, d3), F32),
        compiler_params=pltpu.CompilerParams(
            dimension_semantics=("arbitrary", "arbitrary"), vmem_limit_bytes=VMEM_LIMIT),
        name="ada",
    )(c, ada_w, ada_b.reshape(depth, 1, d3))


def _even_kernel(x_ref, ada_ref, ng_ref, win_ref, pw_ref, ps_ref, sg_ref, sw_ref, sbt_ref, wout_ref,
                 *rest, ts, rider_rows):
    n_rider = len(rider_rows)
    rider_in, o_ref, rider_out = rest[:n_rider], rest[n_rider], rest[n_rider + 1:2 * n_rider + 1]
    abuf, s2buf, s4buf, s8buf, s16buf = rest[2 * n_rider + 1:]
    s = pl.program_id(1)

    step = pl.program_id(0) * pl.num_programs(1) + s
    for src, dst, rows in zip(rider_in, rider_out, rider_rows):
        br = src.shape[0]
        blk = jnp.minimum(step, -(-rows // br) - 1)
        row = blk * br + lax.broadcasted_iota(jnp.int32, src.shape, 0)
        dst[...] = jnp.where(row < rows, src[...], 0.0).astype(dst.dtype)

    d = x_ref.shape[-1]
    hl = POOL_HALO
    g = POOL_GROUP

    @pl.when(s == 0)
    def _():
        abuf[0:hl, :] = jnp.zeros((hl, d), F32)

    @pl.when(s > 0)
    def _():
        abuf[0:hl, :] = abuf[ts:ts + hl, :]

    shift = ada_ref[0, 0, 0:1, :]
    scale = ada_ref[0, 0, 1:2, :]
    h = (_rms(x_ref[0], ng_ref[...] * (1.0 + scale)) + shift).astype(BF16)

    a = _dot(h, win_ref[:, 0:d])
    abuf[hl:hl + ts, :] = a
    u = _dot(h, win_ref[:, d:2 * d])

    n2 = ts + hl - 8
    s2buf[8:8 + n2, :] = abuf[8:8 + n2, :] + abuf[7:7 + n2, :]
    n4 = ts + hl - 16
    s4buf[16:16 + n4, :] = s2buf[16:16 + n4, g:] + s2buf[14:14 + n4, g:]
    n8 = ts + hl - 24
    s8buf[24:24 + n8, :] = s4buf[24:24 + n8, g:] + s4buf[20:20 + n8, g:]
    s16buf[hl:hl + ts, :] = s8buf[hl:hl + ts, g:] + s8buf[hl - 8:hl - 8 + ts, g:]
    v = _dot(h, win_ref[:, 2 * d:3 * d])

    t_glob = s * ts + lax.broadcasted_iota(jnp.int32, (ts, 1), 0)
    win_sums = (s2buf[hl:hl + ts, 0:g], s4buf[hl:hl + ts, 0:g], s8buf[hl:hl + ts, 0:g], s16buf[hl:hl + ts, :])
    ya = []
    for gi, w in enumerate(POOL_WINDOWS):
        count = jnp.minimum(t_glob + 1, w).astype(F32)
        p = win_sums[gi] / count - a[:, gi * g:(gi + 1) * g]
        ya.append(_dot(p.astype(BF16), pw_ref[gi]))
    gate_a = _silu(_dot(h, win_ref[:, 3 * d:4 * d]))
    ya = jnp.concatenate(ya, axis=1) * ps_ref[...] * gate_a

    vn = _rms(v, sg_ref[...]).astype(BF16)
    gate_b = _silu(_dot(h, win_ref[:, 4 * d:5 * d]))
    nblk = ts // SGU_LEN
    row = lax.broadcasted_iota(jnp.int32, (SGU_LEN, SGU_LEN), 0)
    col = lax.broadcasted_iota(jnp.int32, (SGU_LEN, SGU_LEN), 1)
    hd = d // SGU_HEADS
    z_cols = []
    for hh in range(SGU_HEADS):
        wm = jnp.where(col <= row, sw_ref[hh], 0.0).astype(BF16)
        vh = jnp.concatenate(
            [vn[n * SGU_LEN:(n + 1) * SGU_LEN, hh * hd:(hh + 1) * hd] for n in range(nblk)], axis=1)
        zh = _dot(wm, vh) + sbt_ref[:, hh:hh + 1]
        z_cols.append(jnp.concatenate([zh[:, n * hd:(n + 1) * hd] for n in range(nblk)], axis=0))
    yb = u * jnp.concatenate(z_cols, axis=1) * gate_b

    y = jnp.concatenate([ya, yb], axis=1).astype(BF16)
    o_ref[0] = x_ref[0] + ada_ref[0, 0, 2:3, :] * _dot(y, wout_ref[...])


def _even_call(x, ada4, layer, norm_g, in_w, pool_w, pool_scale, sgu_norm_g, sgu_w, sgu_bt, out_w, riders):
    b, sq, d = x.shape
    ts = TS_EVEN
    g = POOL_GROUP
    rows = ts + POOL_HALO
    nj = sq // ts
    rider_blocks = [_rider_blocks(r.shape[0], b * nj) for r in riders]

    def rider_spec(r, br, nblk):
        return pl.BlockSpec((br, r.shape[1]), lambda i, j: (jnp.minimum(i * nj + j, nblk - 1), 0))

    rider_specs = [rider_spec(r, br, nblk) for r, (br, nblk) in zip(riders, rider_blocks)]
    return pl.pallas_call(
        functools.partial(_even_kernel, ts=ts, rider_rows=tuple(r.shape[0] for r in riders)),
        grid=(b, nj),
        in_specs=[
            pl.BlockSpec((1, ts, d), lambda i, j: (i, j, 0)),
            pl.BlockSpec((1, 1, 3, d), lambda i, j: (layer, i, 0, 0)),
            _resident(norm_g.shape),
            _resident(in_w.shape),
            _resident(pool_w.shape),
            _resident(pool_scale.shape),
            _resident(sgu_norm_g.shape),
            _resident(sgu_w.shape),
            _resident(sgu_bt.shape),
            _resident(out_w.shape),
        ] + rider_specs,
        out_specs=[pl.BlockSpec((1, ts, d), lambda i, j: (i, j, 0))] + rider_specs,
        out_shape=[jax.ShapeDtypeStruct(x.shape, F32)] + [
            jax.ShapeDtypeStruct((br * nblk, r.shape[1]), BF16) for r, (br, nblk) in zip(riders, rider_blocks)],
        scratch_shapes=[
            pltpu.VMEM((rows, d), F32),
            pltpu.VMEM((rows, d), F32),
            pltpu.VMEM((rows, d - g), F32),
            pltpu.VMEM((rows, d - 2 * g), F32),
            pltpu.VMEM((rows, d - 3 * g), F32),
        ],
        compiler_params=pltpu.CompilerParams(
            dimension_semantics=("arbitrary", "arbitrary"), vmem_limit_bytes=VMEM_LIMIT),
        name="even_layer",
    )(x, ada4, norm_g, in_w, pool_w, pool_scale, sgu_norm_g, sgu_w, sgu_bt, out_w, *riders)


def _odd_kernel(x_ref, ada_ref, ng_ref, wt_ref, gw_ref, gb_ref, gng_ref, wout_ref, fg_ref,
                o_ref, state, kbuf, vtbuf, qtbuf, acc, crow, ybuf, gbuf, *, ts):
    s = pl.program_id(1)

    @pl.when(s == 0)
    def _():
        state[...] = jnp.zeros(state.shape, F32)

    shift = ada_ref[0, 0, 0:1, :]
    scale = ada_ref[0, 0, 1:2, :]
    h = (_rms(x_ref[0], ng_ref[...] * (1.0 + scale)) + shift).astype(BF16)

    kw = GLA_HEADS * GLA_DK
    vw = GLA_HEADS * GLA_DV
    sbw = SB_HEADS * SB_DH
    r_gq, r_gk, r_gv = 0, kw, 2 * kw
    r_glr = r_gv + vw
    r_sq = r_glr + GLA_RANK
    r_sk, r_sv, r_gate = r_sq + sbw, r_sq + 2 * sbw, r_sq + 3 * sbw

    def proj(r0, n):
        return _dot_nt(h, wt_ref[r0:r0 + n, :])

    sb_scale = SB_DH ** -0.5
    s0 = pl.multiple_of(s * ts, ts)

    glr = proj(r_glr, GLA_RANK)
    gq = proj(r_gq, kw) * (GLA_DK ** -0.5)
    glr_hi, glr_lo = _split2(glr)
    gw_hi, gw_lo = _split2(gw_ref[...])
    pre = _dot(jnp.concatenate([glr_hi, glr_lo, glr_hi], axis=1),
               jnp.concatenate([gw_hi, gw_hi, gw_lo], axis=0)) + gb_ref[...]
    gk = proj(r_gk, kw)
    la = -_softplus(-pre) * (1.0 / GLA_TAU)
    gv_b = proj(r_gv, vw).astype(BF16)
    r = lax.broadcasted_iota(jnp.int32, (ts, ts), 0)
    cc = lax.broadcasted_iota(jnp.int32, (ts, ts), 1)
    same_chunk_causal = (cc <= r) & (cc // GLA_CHUNK == r // GLA_CHUNK)
    tri = jnp.where(same_chunk_causal, 1.0, 0.0).astype(BF16)
    la_hi, la_lo = _split2(la)
    sq = proj(r_sq, sbw) * sb_scale
    bcum = _dot(jnp.concatenate([tri, tri], axis=1), jnp.concatenate([la_hi, la_lo], axis=0))
    sk = proj(r_sk, sbw).astype(BF16)
    sv = proj(r_sv, sbw)
    for hh in range(SB_HEADS):
        hs = slice(hh * SB_DH, (hh + 1) * SB_DH)
        qtbuf[hh] = sq[:, hs].T.astype(BF16)
        kbuf[hh, pl.ds(s0, ts), :] = sk[:, hs]
        vtbuf[hh, s] = sv[:, hs].T.astype(BF16)

    nchunk = ts // GLA_CHUNK
    q_dec = (gq * jnp.exp(bcum)).astype(BF16)
    k_inv = (gk * jnp.exp(-bcum)).astype(BF16)
    b_last = jnp.concatenate(
        [jnp.broadcast_to(bcum[(ci + 1) * GLA_CHUNK - 1:(ci + 1) * GLA_CHUNK, :], (GLA_CHUNK, kw))
         for ci in range(nchunk)], axis=0)
    k_end = gk * jnp.exp(b_last - bcum)

    heads = range(GLA_HEADS)
    kss = [slice(hh * GLA_DK, (hh + 1) * GLA_DK) for hh in heads]
    vss = [slice(hh * GLA_DV, (hh + 1) * GLA_DV) for hh in heads]
    zs = [_dot_nt(q_dec[:, kss[hh]], k_inv[:, kss[hh]]) for hh in heads]
    lane_chunk = lax.broadcasted_iota(jnp.int32, (GLA_DK, ts), 1) // GLA_CHUNK
    us, bcts = [], []
    for hh in heads:
        ket = k_end[:, kss[hh]].T
        bcts.append(bcum[:, kss[hh]].T)
        lhs = jnp.concatenate(
            [jnp.where(lane_chunk == ci, ket, 0.0).astype(BF16) for ci in range(nchunk)], axis=0)
        us.append(_dot(lhs, gv_b[:, vss[hh]]))
    zero_blk = jnp.zeros((GLA_CHUNK, GLA_DK), BF16)
    for hh in heads:
        att = jnp.where(same_chunk_causal, zs[hh], 0.0).astype(BF16)
        st = state[hh]
        starts = []
        for ci in range(nchunk):
            starts.append(st.astype(BF16))
            last = (ci + 1) * GLA_CHUNK - 1
            st = jnp.exp(bcts[hh][:, last:last + 1]) * st + us[hh][ci * GLA_DK:(ci + 1) * GLA_DK, :]
        state[hh] = st
        qh = q_dec[:, kss[hh]]
        q_blocks = jnp.concatenate(
            [jnp.concatenate([qh[ci * GLA_CHUNK:(ci + 1) * GLA_CHUNK, :] if cj == ci else zero_blk
                              for cj in range(nchunk)], axis=1) for ci in range(nchunk)], axis=0)
        o = _dot(jnp.concatenate([att, q_blocks], axis=1),
                 jnp.concatenate([gv_b[:, vss[hh]]] + starts, axis=0))
        ybuf[:, vss[hh]] = _rms(o, gng_ref[:, vss[hh]])

    groups = [tuple(range(g, g + SB_GROUP)) for g in range(0, SB_HEADS, SB_GROUP)]
    gate_chunk = (vw + sbw) // (SB_STAGES * len(groups))

    def gate_filler(ci):
        def run():
            lo = ci * gate_chunk
            gbuf[:, lo:lo + gate_chunk] = _silu(proj(r_gate + lo, gate_chunk))
        return run

    fillers = [gate_filler(ci) for ci in range(SB_STAGES * len(groups))]

    jr = lax.broadcasted_iota(jnp.int32, (ts, ts), 0)
    jc = lax.broadcasted_iota(jnp.int32, (ts, ts), 1)
    later = jnp.where(jc > jr, 1.0, 0.0).astype(BF16)
    valid = jr < jc

    def sb_blocks(heads, kbs, fresh, penalty=None, between=()):
        between = list(between)

        def stage_done():
            if between:
                between.pop(0)()

        zs = [[_dot(kbuf[hh, pl.ds(pl.multiple_of(kb * ts, ts), ts), :], qtbuf[hh]) for hh in heads] for kb in kbs]
        stage_done()
        sps, lbetas, firsts = [], [], []
        for bi, zb in enumerate(zs):
            sps.append([])
            lbetas.append([])
            firsts.append([])
            for z in zb:
                lg = jnp.log(1.0 + jnp.exp2(jnp.abs(z) * (-LOG2E)))
                sp = jnp.maximum(z, 0.0) + lg
                if fresh and bi == 0:
                    sp = jnp.where(valid, sp, 0.0)
                sps[bi].append(sp.astype(BF16))
                firsts[bi].append(sp[0:1, :])
                lbetas[bi].append(jnp.minimum(z, 0.0) - lg)
        stage_done()
        sums = [[_dot(later, sp) for sp in spb] for spb in sps]
        stage_done()
        ws, carries = [], []
        for hi, hh in enumerate(heads):
            carry = None if fresh else crow[hh]
            ws.append([])
            for bi in range(len(kbs)):
                xarg = lbetas[bi][hi] - sums[bi][hi]
                if carry is not None:
                    xarg = xarg - carry
                w = jnp.exp(xarg)
                if fresh and bi == 0:
                    w = jnp.where(valid, w, 0.0)
                ws[hi].append(w.astype(BF16))
                total = sums[bi][hi][0:1, :] + firsts[bi][hi]
                carry = total if carry is None else carry + total
                if penalty is not None and bi == 0:
                    carry = carry + penalty
            carries.append(carry)
        stage_done()
        for hi, hh in enumerate(heads):
            contrib = _dot(jnp.concatenate([vtbuf[hh, kb] for kb in kbs], axis=1),
                           jnp.concatenate(ws[hi], axis=0))
            acc[hh] = contrib if fresh else acc[hh] + contrib
            crow[hh] = carries[hi]

    def any_alive():
        low = crow[0]
        for hh in range(1, SB_HEADS):
            low = jnp.minimum(low, crow[hh])
        return (jnp.min(low) < SB_DEAD_CARRY).astype(jnp.int32)

    penalty = jnp.where(s > 0, 0.0, SB_NO_BLOCK)
    for gi, heads in enumerate(groups):
        sb_blocks(heads, (s, jnp.maximum(s - 1, 0)), True, penalty, fillers[gi * SB_STAGES:(gi + 1) * SB_STAGES])

    def more_blocks(state_):
        i, alive = state_
        return jnp.logical_and(i < s, alive > 0)

    def key_block(state_):
        i, _ = state_
        for heads in groups:
            sb_blocks(heads, (s - 1 - i,), False)
        return i + 1, any_alive()

    lax.while_loop(more_blocks, key_block, (jnp.int32(1), any_alive()))

    for hh in range(SB_HEADS):
        ybuf[:, vw + hh * SB_DH:vw + (hh + 1) * SB_DH] = acc[hh].T

    y = (ybuf[...] * gbuf[...]).astype(BF16)
    x1 = x_ref[0] + ada_ref[0, 0, 2:3, :] * _dot(y, wout_ref[...])
    o_ref[0] = _rms(x1, fg_ref[...])


def _odd_call(x, ada4, layer, norm_g, in_wt, gate_w, gate_b, gla_norm_g, out_w, final_g):
    b, sq, d = x.shape
    ts = TS_ODD
    return pl.pallas_call(
        functools.partial(_odd_kernel, ts=ts),
        grid=(b, sq // ts),
        in_specs=[
            pl.BlockSpec((1, ts, d), lambda i, j: (i, j, 0)),
            pl.BlockSpec((1, 1, 3, d), lambda i, j: (layer, i, 0, 0)),
            _resident(norm_g.shape),
            _resident(in_wt.shape),
            _resident(gate_w.shape),
            _resident(gate_b.shape),
            _resident(gla_norm_g.shape),
            _resident(out_w.shape),
            _resident(final_g.shape),
        ],
        out_specs=pl.BlockSpec((1, ts, d), lambda i, j: (i, j, 0)),
        out_shape=jax.ShapeDtypeStruct(x.shape, F32),
        scratch_shapes=[
            pltpu.VMEM((GLA_HEADS, GLA_DK, GLA_DV), F32),
            pltpu.VMEM((SB_HEADS, sq, SB_DH), BF16),
            pltpu.VMEM((SB_HEADS, sq // ts, SB_DH, ts), BF16),
            pltpu.VMEM((SB_HEADS, SB_DH, ts), BF16),
            pltpu.VMEM((SB_HEADS, SB_DH, ts), F32),
            pltpu.VMEM((SB_HEADS, 1, ts), F32),
            pltpu.VMEM((ts, 2 * d), F32),
            pltpu.VMEM((ts, 2 * d), F32),
        ],
        compiler_params=pltpu.CompilerParams(
            dimension_semantics=("arbitrary", "arbitrary"), vmem_limit_bytes=VMEM_LIMIT),
        name="odd_layer",
    )(x, ada4, norm_g, in_wt, gate_w, gate_b, gla_norm_g, out_w, final_g)


def kernel(x, c, ada_w, ada_b, norm_g, even_in_w, pool_w, pool_scale, sgu_norm_g, sgu_w, sgu_b, even_out_w,
           odd_in_w, gla_gate_w, gla_gate_b, gla_norm_g, odd_out_w, final_g):
    depth, d, _ = ada_w.shape
    assert depth == 2 and even_in_w.shape[0] == 1 and odd_in_w.shape[0] == 1
    b = x.shape[0]
    ada4 = _ada_call(c, ada_w, ada_b).reshape(depth, b, 3, d)

    x, in_wt, odd_out = _even_call(
        x, ada4, 0, norm_g[0:1], even_in_w[0].astype(BF16), pool_w[0].astype(BF16), pool_scale[0:1],
        sgu_norm_g[0:1], sgu_w[0], sgu_b[0].T, even_out_w[0].astype(BF16),
        [jnp.swapaxes(odd_in_w[0], 0, 1), odd_out_w[0]])
    return _odd_call(
        x, ada4, 1, norm_g[1:2], in_wt, gla_gate_w[0], gla_gate_b[0:1], gla_norm_g[0].reshape(1, -1),
        odd_out, final_g.reshape(1, -1))
```

```python
import functools

import jax
import jax.numpy as jnp
from jax import lax
from jax.experimental import pallas as pl
from jax.experimental.pallas import tpu as pltpu

F32 = jnp.float32
BF16 = jnp.bfloat16

EPS = 1e-6
LOG2E = 1.4426950408889634
POOL_WINDOWS = (2, 4, 8, 16)
POOL_GROUP = 256
POOL_HALO = 32
SGU_LEN = 128
SGU_HEADS = 8
GLA_HEADS = 4
GLA_DK = 128
GLA_DV = 256
GLA_CHUNK = 64
GLA_RANK = 16
GLA_TAU = 16.0
SB_HEADS = 8
SB_DH = 128
SB_GROUP = 4
SB_STAGES = 4
SB_DEAD_CARRY = 105.0
SB_NO_BLOCK = 1.0e4

TS_EVEN = 512
TS_ODD = 256
VMEM_LIMIT = 56 * 1024 * 1024


def _dot(a, b):
    return jnp.dot(a, b, preferred_element_type=F32)


def _dot_nt(a, b):
    return lax.dot_general(a, b, (((1,), (1,)), ((), ())), preferred_element_type=F32)


def _split2(x):
    hi = x.astype(BF16)
    lo = (x - hi.astype(F32)).astype(BF16)
    return hi, lo


def _sigmoid(x):
    return 1.0 / (1.0 + jnp.exp(-x))


def _silu(x):
    return x * _sigmoid(x)


def _softplus(x):
    return jnp.maximum(x, 0.0) + jnp.log(1.0 + jnp.exp(-jnp.abs(x)))


def _rms(x, g):
    return x * lax.rsqrt(jnp.mean(x * x, axis=-1, keepdims=True) + EPS) * g


def _resident(shape):
    n = len(shape)
    return pl.BlockSpec(shape, lambda *_: (0,) * n, pipeline_mode=pl.Buffered(1))


def _rider_blocks(rows, steps):
    br = -(-rows // steps)
    br = -(-br // 16) * 16
    return br, -(-rows // br)


def _ada_kernel(c_ref, w_ref, b_ref, o_ref):
    sc = _silu(c_ref[...])
    o_ref[0] = jnp.dot(sc, w_ref[0], preferred_element_type=F32,
                       precision=lax.Precision.HIGHEST) + b_ref[0]


def _ada_call(c, ada_w, ada_b):
    depth, d, d3 = ada_w.shape
    b = c.shape[0]
    tn = 1024
    return pl.pallas_call(
        _ada_kernel,
        grid=(depth, d3 // tn),
        in_specs=[
            pl.BlockSpec((b, d), lambda i, j: (0, 0)),
            pl.BlockSpec((1, d, tn), lambda i, j: (i, 0, j)),
            pl.BlockSpec((1, 1, tn), lambda i, j: (i, 0, j)),
        ],
        out_specs=pl.BlockSpec((1, b, tn), lambda i, j: (i, 0, j)),
        out_shape=jax.ShapeDtypeStruct((depth, b, d3), F32),
        compiler_params=pltpu.CompilerParams(
            dimension_semantics=("arbitrary", "arbitrary"), vmem_limit_bytes=VMEM_LIMIT),
        name="ada",
    )(c, ada_w, ada_b.reshape(depth, 1, d3))


def _even_kernel(x_ref, ada_ref, ng_ref, win_ref, pw_ref, ps_ref, sg_ref, sw_ref, sbt_ref, wout_ref,
                 *rest, ts, rider_rows):
    n_rider = len(rider_rows)
    rider_in, o_ref, rider_out = rest[:n_rider], rest[n_rider], rest[n_rider + 1:2 * n_rider + 1]
    abuf, s2buf, s4buf, s8buf, s16buf = rest[2 * n_rider + 1:]
    s = pl.program_id(1)

    step = pl.program_id(0) * pl.num_programs(1) + s
    for src, dst, rows in zip(rider_in, rider_out, rider_rows):
        br = src.shape[0]
        blk = jnp.minimum(step, -(-rows // br) - 1)
        row = blk * br + lax.broadcasted_iota(jnp.int32, src.shape, 0)
        dst[...] = jnp.where(row < rows, src[...], 0.0).astype(dst.dtype)

    d = x_ref.shape[-1]
    hl = POOL_HALO
    g = POOL_GROUP

    @pl.when(s == 0)
    def _():
        abuf[0:hl, :] = jnp.zeros((hl, d), F32)

    @pl.when(s > 0)
    def _():
        abuf[0:hl, :] = abuf[ts:ts + hl, :]

    shift = ada_ref[0, 0, 0:1, :]
    scale = ada_ref[0, 0, 1:2, :]
    h = (_rms(x_ref[0], ng_ref[...] * (1.0 + scale)) + shift).astype(BF16)

    a = _dot(h, win_ref[:, 0:d])
    abuf[hl:hl + ts, :] = a
    u = _dot(h, win_ref[:, d:2 * d])

    n2 = ts + hl - 8
    s2buf[8:8 + n2, :] = abuf[8:8 + n2, :] + abuf[7:7 + n2, :]
    n4 = ts + hl - 16
    s4buf[16:16 + n4, :] = s2buf[16:16 + n4, g:] + s2buf[14:14 + n4, g:]
    n8 = ts + hl - 24
    s8buf[24:24 + n8, :] = s4buf[24:24 + n8, g:] + s4buf[20:20 + n8, g:]
    s16buf[hl:hl + ts, :] = s8buf[hl:hl + ts, g:] + s8buf[hl - 8:hl - 8 + ts, g:]
    v = _dot(h, win_ref[:, 2 * d:3 * d])

    t_glob = s * ts + lax.broadcasted_iota(jnp.int32, (ts, 1), 0)
    win_sums = (s2buf[hl:hl + ts, 0:g], s4buf[hl:hl + ts, 0:g], s8buf[hl:hl + ts, 0:g], s16buf[hl:hl + ts, :])
    ya = []
    for gi, w in enumerate(POOL_WINDOWS):
        count = jnp.minimum(t_glob + 1, w).astype(F32)
        p = win_sums[gi] / count - a[:, gi * g:(gi + 1) * g]
        ya.append(_dot(p.astype(BF16), pw_ref[gi]))
    gate_a = _silu(_dot(h, win_ref[:, 3 * d:4 * d]))
    ya = jnp.concatenate(ya, axis=1) * ps_ref[...] * gate_a

    vn = _rms(v, sg_ref[...]).astype(BF16)
    gate_b = _silu(_dot(h, win_ref[:, 4 * d:5 * d]))
    nblk = ts // SGU_LEN
    row = lax.broadcasted_iota(jnp.int32, (SGU_LEN, SGU_LEN), 0)
    col = lax.broadcasted_iota(jnp.int32, (SGU_LEN, SGU_LEN), 1)
    hd = d // SGU_HEADS
    z_cols = []
    for hh in range(SGU_HEADS):
        wm = jnp.where(col <= row, sw_ref[hh], 0.0).astype(BF16)
        vh = jnp.concatenate(
            [vn[n * SGU_LEN:(n + 1) * SGU_LEN, hh * hd:(hh + 1) * hd] for n in range(nblk)], axis=1)
        zh = _dot(wm, vh) + sbt_ref[:, hh:hh + 1]
        z_cols.append(jnp.concatenate([zh[:, n * hd:(n + 1) * hd] for n in range(nblk)], axis=0))
    yb = u * jnp.concatenate(z_cols, axis=1) * gate_b

    y = jnp.concatenate([ya, yb], axis=1).astype(BF16)
    o_ref[0] = x_ref[0] + ada_ref[0, 0, 2:3, :] * _dot(y, wout_ref[...])


def _even_call(x, ada4, layer, norm_g, in_w, pool_w, pool_scale, sgu_norm_g, sgu_w, sgu_bt, out_w, riders):
    b, sq, d = x.shape
    ts = TS_EVEN
    g = POOL_GROUP
    rows = ts + POOL_HALO
    nj = sq // ts
    rider_blocks = [_rider_blocks(r.shape[0], b * nj) for r in riders]

    def rider_spec(r, br, nblk):
        return pl.BlockSpec((br, r.shape[1]), lambda i, j: (jnp.minimum(i * nj + j, nblk - 1), 0))

    rider_specs = [rider_spec(r, br, nblk) for r, (br, nblk) in zip(riders, rider_blocks)]
    return pl.pallas_call(
        functools.partial(_even_kernel, ts=ts, rider_rows=tuple(r.shape[0] for r in riders)),
        grid=(b, nj),
        in_specs=[
            pl.BlockSpec((1, ts, d), lambda i, j: (i, j, 0)),
            pl.BlockSpec((1, 1, 3, d), lambda i, j: (layer, i, 0, 0)),
            _resident(norm_g.shape),
            _resident(in_w.shape),
            _resident(pool_w.shape),
            _resident(pool_scale.shape),
            _resident(sgu_norm_g.shape),
            _resident(sgu_w.shape),
            _resident(sgu_bt.shape),
            _resident(out_w.shape),
        ] + rider_specs,
        out_specs=[pl.BlockSpec((1, ts, d), lambda i, j: (i, j, 0))] + rider_specs,
        out_shape=[jax.ShapeDtypeStruct(x.shape, F32)] + [
            jax.ShapeDtypeStruct((br * nblk, r.shape[1]), BF16) for r, (br, nblk) in zip(riders, rider_blocks)],
        scratch_shapes=[
            pltpu.VMEM((rows, d), F32),
            pltpu.VMEM((rows, d), F32),
            pltpu.VMEM((rows, d - g), F32),
            pltpu.VMEM((rows, d - 2 * g), F32),
            pltpu.VMEM((rows, d - 3 * g), F32),
        ],
        compiler_params=pltpu.CompilerParams(
            dimension_semantics=("arbitrary", "arbitrary"), vmem_limit_bytes=VMEM_LIMIT),
        name="even_layer",
    )(x, ada4, norm_g, in_w, pool_w, pool_scale, sgu_norm_g, sgu_w, sgu_bt, out_w, *riders)


def _odd_kernel(x_ref, ada_ref, ng_ref, wt_ref, gw_ref, gb_ref, gng_ref, wout_ref, fg_ref,
                o_ref, state, kbuf, vtbuf, qtbuf, acc, crow, ybuf, gbuf, *, ts):
    s = pl.program_id(1)

    @pl.when(s == 0)
    def _():
        state[...] = jnp.zeros(state.shape, F32)

    shift = ada_ref[0, 0, 0:1, :]
    scale = ada_ref[0, 0, 1:2, :]
    h = (_rms(x_ref[0], ng_ref[...] * (1.0 + scale)) + shift).astype(BF16)

    kw = GLA_HEADS * GLA_DK
    vw = GLA_HEADS * GLA_DV
    sbw = SB_HEADS * SB_DH
    r_gq, r_gk, r_gv = 0, kw, 2 * kw
    r_glr = r_gv + vw
    r_sq = r_glr + GLA_RANK
    r_sk, r_sv, r_gate = r_sq + sbw, r_sq + 2 * sbw, r_sq + 3 * sbw

    def proj(r0, n):
        return _dot_nt(h, wt_ref[r0:r0 + n, :])

    sb_scale = SB_DH ** -0.5
    s0 = pl.multiple_of(s * ts, ts)

    glr = proj(r_glr, GLA_RANK)
    gq = proj(r_gq, kw) * (GLA_DK ** -0.5)
    glr_hi, glr_lo = _split2(glr)
    gw_hi, gw_lo = _split2(gw_ref[...])
    pre = _dot(jnp.concatenate([glr_hi, glr_lo, glr_hi], axis=1),
               jnp.concatenate([gw_hi, gw_hi, gw_lo], axis=0)) + gb_ref[...]
    gk = proj(r_gk, kw)
    la = -_softplus(-pre) * (1.0 / GLA_TAU)
    gv_b = proj(r_gv, vw).astype(BF16)
    r = lax.broadcasted_iota(jnp.int32, (ts, ts), 0)
    cc = lax.broadcasted_iota(jnp.int32, (ts, ts), 1)
    same_chunk_causal = (cc <= r) & (cc // GLA_CHUNK == r // GLA_CHUNK)
    tri = jnp.where(same_chunk_causal, 1.0, 0.0).astype(BF16)
    la_hi, la_lo = _split2(la)
    sq = proj(r_sq, sbw) * sb_scale
    bcum = _dot(jnp.concatenate([tri, tri], axis=1), jnp.concatenate([la_hi, la_lo], axis=0))
    sk = proj(r_sk, sbw).astype(BF16)
    sv = proj(r_sv, sbw)
    for hh in range(SB_HEADS):
        hs = slice(hh * SB_DH, (hh + 1) * SB_DH)
        qtbuf[hh] = sq[:, hs].T.astype(BF16)
        kbuf[hh, pl.ds(s0, ts), :] = sk[:, hs]
        vtbuf[hh, s] = sv[:, hs].T.astype(BF16)

    nchunk = ts // GLA_CHUNK
    q_dec = (gq * jnp.exp(bcum)).astype(BF16)
    k_inv = (gk * jnp.exp(-bcum)).astype(BF16)
    b_last = jnp.concatenate(
        [jnp.broadcast_to(bcum[(ci + 1) * GLA_CHUNK - 1:(ci + 1) * GLA_CHUNK, :], (GLA_CHUNK, kw))
         for ci in range(nchunk)], axis=0)
    k_end = gk * jnp.exp(b_last - bcum)

    heads = range(GLA_HEADS)
    kss = [slice(hh * GLA_DK, (hh + 1) * GLA_DK) for hh in heads]
    vss = [slice(hh * GLA_DV, (hh + 1) * GLA_DV) for hh in heads]
    zs = [_dot_nt(q_dec[:, kss[hh]], k_inv[:, kss[hh]]) for hh in heads]
    lane_chunk = lax.broadcasted_iota(jnp.int32, (GLA_DK, ts), 1) // GLA_CHUNK
    us, bcts = [], []
    for hh in heads:
        ket = k_end[:, kss[hh]].T
        bcts.append(bcum[:, kss[hh]].T)
        lhs = jnp.concatenate(
            [jnp.where(lane_chunk == ci, ket, 0.0).astype(BF16) for ci in range(nchunk)], axis=0)
        us.append(_dot(lhs, gv_b[:, vss[hh]]))
    for hh in heads:
        att = jnp.where(same_chunk_causal, zs[hh], 0.0).astype(BF16)
        st = state[hh]
        starts = []
        for ci in range(nchunk):
            starts.append(st.astype(BF16))
            last = (ci + 1) * GLA_CHUNK - 1
            st = jnp.exp(bcts[hh][:, last:last + 1]) * st + us[hh][ci * GLA_DK:(ci + 1) * GLA_DK, :]
        state[hh] = st
        qh = q_dec[:, kss[hh]]
        o = _dot(att, gv_b[:, vss[hh]]) + jnp.concatenate(
            [_dot(qh[ci * GLA_CHUNK:(ci + 1) * GLA_CHUNK, :], starts[ci]) for ci in range(nchunk)], axis=0)
        ybuf[:, vss[hh]] = _rms(o, gng_ref[:, vss[hh]])

    groups = [tuple(range(g, g + SB_GROUP)) for g in range(0, SB_HEADS, SB_GROUP)]
    gate_chunk = (vw + sbw) // (SB_STAGES * len(groups))

    def gate_filler(ci):
        def run():
            lo = ci * gate_chunk
            gbuf[:, lo:lo + gate_chunk] = _silu(proj(r_gate + lo, gate_chunk))
        return run

    fillers = [gate_filler(ci) for ci in range(SB_STAGES * len(groups))]

    jr = lax.broadcasted_iota(jnp.int32, (ts, ts), 0)
    jc = lax.broadcasted_iota(jnp.int32, (ts, ts), 1)
    later = jnp.where(jc > jr, 1.0, 0.0).astype(BF16)
    valid = jr < jc

    def sb_blocks(heads, kbs, fresh, penalty=None, between=()):
        between = list(between)

        def stage_done():
            if between:
                between.pop(0)()

        zs = [[_dot(kbuf[hh, pl.ds(pl.multiple_of(kb * ts, ts), ts), :], qtbuf[hh]) for hh in heads] for kb in kbs]
        stage_done()
        sps, lbetas, firsts = [], [], []
        for bi, zb in enumerate(zs):
            sps.append([])
            lbetas.append([])
            firsts.append([])
            for z in zb:
                lg = jnp.log(1.0 + jnp.exp2(jnp.abs(z) * (-LOG2E)))
                sp = jnp.maximum(z, 0.0) + lg
                if fresh and bi == 0:
                    sp = jnp.where(valid, sp, 0.0)
                sps[bi].append(sp.astype(BF16))
                firsts[bi].append(sp[0:1, :])
                lbetas[bi].append(jnp.minimum(z, 0.0) - lg)
        stage_done()
        sums = [[_dot(later, sp) for sp in spb] for spb in sps]
        stage_done()
        ws, carries = [], []
        for hi, hh in enumerate(heads):
            carry = None if fresh else crow[hh]
            ws.append([])
            for bi in range(len(kbs)):
                xarg = lbetas[bi][hi] - sums[bi][hi]
                if carry is not None:
                    xarg = xarg - carry
                w = jnp.exp(xarg)
                if fresh and bi == 0:
                    w = jnp.where(valid, w, 0.0)
                ws[hi].append(w.astype(BF16))
                total = sums[bi][hi][0:1, :] + firsts[bi][hi]
                carry = total if carry is None else carry + total
                if penalty is not None and bi == 0:
                    carry = carry + penalty
            carries.append(carry)
        stage_done()
        for hi, hh in enumerate(heads):
            contrib = _dot(jnp.concatenate([vtbuf[hh, kb] for kb in kbs], axis=1),
                           jnp.concatenate(ws[hi], axis=0))
            acc[hh] = contrib if fresh else acc[hh] + contrib
            crow[hh] = carries[hi]

    def any_alive():
        low = crow[0]
        for hh in range(1, SB_HEADS):
            low = jnp.minimum(low, crow[hh])
        return (jnp.min(low) < SB_DEAD_CARRY).astype(jnp.int32)

    penalty = jnp.where(s > 0, 0.0, SB_NO_BLOCK)
    for gi, heads in enumerate(groups):
        sb_blocks(heads, (s, jnp.maximum(s - 1, 0)), True, penalty, fillers[gi * SB_STAGES:(gi + 1) * SB_STAGES])

    def more_blocks(state_):
        i, alive = state_
        return jnp.logical_and(i < s, alive > 0)

    def key_block(state_):
        i, _ = state_
        for heads in groups:
            sb_blocks(heads, (s - 1 - i,), False)
        return i + 1, any_alive()

    lax.while_loop(more_blocks, key_block, (jnp.int32(1), any_alive()))

    for hh in range(SB_HEADS):
        ybuf[:, vw + hh * SB_DH:vw + (hh + 1) * SB_DH] = acc[hh].T

    y = (ybuf[...] * gbuf[...]).astype(BF16)
    x1 = x_ref[0] + ada_ref[0, 0, 2:3, :] * _dot(y, wout_ref[...])
    o_ref[0] = _rms(x1, fg_ref[...])


def _odd_call(x, ada4, layer, norm_g, in_wt, gate_w, gate_b, gla_norm_g, out_w, final_g):
    b, sq, d = x.shape
    ts = TS_ODD
    return pl.pallas_call(
        functools.partial(_odd_kernel, ts=ts),
        grid=(b, sq // ts),
        in_specs=[
            pl.BlockSpec((1, ts, d), lambda i, j: (i, j, 0)),
            pl.BlockSpec((1, 1, 3, d), lambda i, j: (layer, i, 0, 0)),
            _resident(norm_g.shape),
            _resident(in_wt.shape),
            _resident(gate_w.shape),
            _resident(gate_b.shape),
            _resident(gla_norm_g.shape),
            _resident(out_w.shape),
            _resident(final_g.shape),
        ],
        out_specs=pl.BlockSpec((1, ts, d), lambda i, j: (i, j, 0)),
        out_shape=jax.ShapeDtypeStruct(x.shape, F32),
        scratch_shapes=[
            pltpu.VMEM((GLA_HEADS, GLA_DK, GLA_DV), F32),
            pltpu.VMEM((SB_HEADS, sq, SB_DH), BF16),
            pltpu.VMEM((SB_HEADS, sq // ts, SB_DH, ts), BF16),
            pltpu.VMEM((SB_HEADS, SB_DH, ts), BF16),
            pltpu.VMEM((SB_HEADS, SB_DH, ts), F32),
            pltpu.VMEM((SB_HEADS, 1, ts), F32),
            pltpu.VMEM((ts, 2 * d), F32),
            pltpu.VMEM((ts, 2 * d), F32),
        ],
        compiler_params=pltpu.CompilerParams(
            dimension_semantics=("arbitrary", "arbitrary"), vmem_limit_bytes=VMEM_LIMIT),
        name="odd_layer",
    )(x, ada4, norm_g, in_wt, gate_w, gate_b, gla_norm_g, out_w, final_g)


def kernel(x, c, ada_w, ada_b, norm_g, even_in_w, pool_w, pool_scale, sgu_norm_g, sgu_w, sgu_b, even_out_w,
           odd_in_w, gla_gate_w, gla_gate_b, gla_norm_g, odd_out_w, final_g):
    depth, d, _ = ada_w.shape
    assert depth == 2 and even_in_w.shape[0] == 1 and odd_in_w.shape[0] == 1
    b = x.shape[0]
    ada4 = _ada_call(c, ada_w, ada_b).reshape(depth, b, 3, d)

    x, in_wt, odd_out = _even_call(
        x, ada4, 0, norm_g[0:1], even_in_w[0].astype(BF16), pool_w[0].astype(BF16), pool_scale[0:1],
        sgu_norm_g[0:1], sgu_w[0], sgu_b[0].T, even_out_w[0].astype(BF16),
        [jnp.swapaxes(odd_in_w[0], 0, 1), odd_out_w[0]])
    return _odd_call(
        x, ada4, 1, norm_g[1:2], in_wt, gla_gate_w[0], gla_gate_b[0:1], gla_norm_g[0].reshape(1, -1),
        odd_out, final_g.reshape(1, -1))
```

```python
import functools

import jax
import jax.numpy as jnp
from jax import lax
from jax.experimental import pallas as pl
from jax.experimental.pallas import tpu as pltpu

F32 = jnp.float32
BF16 = jnp.bfloat16

EPS = 1e-6
LOG2E = 1.4426950408889634
POOL_WINDOWS = (2, 4, 8, 16)
POOL_GROUP = 256
POOL_HALO = 32
SGU_LEN = 128
SGU_HEADS = 8
GLA_HEADS = 4
GLA_DK = 128
GLA_DV = 256
GLA_CHUNK = 64
GLA_RANK = 16
GLA_TAU = 16.0
SB_HEADS = 8
SB_DH = 128
SB_GROUP = 4
SB_STAGES = 4
SB_DEAD_CARRY = 105.0
SB_NO_BLOCK = 1.0e4

TS_EVEN = 512
TS_ODD = 256
VMEM_LIMIT = 56 * 1024 * 1024


def _dot(a, b):
    return jnp.dot(a, b, preferred_element_type=F32)


def _dot_nt(a, b):
    return lax.dot_general(a, b, (((1,), (1,)), ((), ())), preferred_element_type=F32)


def _split2(x):
    hi = x.astype(BF16)
    lo = (x - hi.astype(F32)).astype(BF16)
    return hi, lo


def _sigmoid(x):
    return 1.0 / (1.0 + jnp.exp(-x))


def _silu(x):
    return x * _sigmoid(x)


def _softplus(x):
    return jnp.maximum(x, 0.0) + jnp.log(1.0 + jnp.exp(-jnp.abs(x)))


def _rms(x, g):
    return x * lax.rsqrt(jnp.mean(x * x, axis=-1, keepdims=True) + EPS) * g


def _resident(shape):
    n = len(shape)
    return pl.BlockSpec(shape, lambda *_: (0,) * n, pipeline_mode=pl.Buffered(1))


def _rider_blocks(rows, steps):
    br = -(-rows // steps)
    br = -(-br // 16) * 16
    return br, -(-rows // br)


def _ada_kernel(c_ref, w_ref, b_ref, o_ref):
    sc = _silu(c_ref[...])
    o_ref[0] = jnp.dot(sc, w_ref[0], preferred_element_type=F32,
                       precision=lax.Precision.HIGHEST) + b_ref[0]


def _ada_call(c, ada_w, ada_b):
    depth, d, d3 = ada_w.shape
    b = c.shape[0]
    tn = 1024
    return pl.pallas_call(
        _ada_kernel,
        grid=(depth, d3 // tn),
        in_specs=[
            pl.BlockSpec((b, d), lambda i, j: (0, 0)),
            pl.BlockSpec((1, d, tn), lambda i, j: (i, 0, j)),
            pl.BlockSpec((1, 1, tn), lambda i, j: (i, 0, j)),
        ],
        out_specs=pl.BlockSpec((1, b, tn), lambda i, j: (i, 0, j)),
        out_shape=jax.ShapeDtypeStruct((depth, b, d3), F32),
        compiler_params=pltpu.CompilerParams(
            dimension_semantics=("arbitrary", "arbitrary"), vmem_limit_bytes=VMEM_LIMIT),
        name="ada",
    )(c, ada_w, ada_b.reshape(depth, 1, d3))


def _even_kernel(x_ref, ada_ref, ng_ref, win_ref, pw_ref, ps_ref, sg_ref, sw_ref, sbt_ref, wout_ref,
                 *rest, ts, rider_rows):
    n_rider = len(rider_rows)
    rider_in, o_ref, rider_out = rest[:n_rider], rest[n_rider], rest[n_rider + 1:2 * n_rider + 1]
    abuf, s2buf, s4buf, s8buf, s16buf = rest[2 * n_rider + 1:]
    s = pl.program_id(1)

    step = pl.program_id(0) * pl.num_programs(1) + s
    for src, dst, rows in zip(rider_in, rider_out, rider_rows):
        br = src.shape[0]
        blk = jnp.minimum(step, -(-rows // br) - 1)
        row = blk * br + lax.broadcasted_iota(jnp.int32, src.shape, 0)
        dst[...] = jnp.where(row < rows, src[...], 0.0).astype(dst.dtype)

    d = x_ref.shape[-1]
    hl = POOL_HALO
    g = POOL_GROUP

    @pl.when(s == 0)
    def _():
        abuf[0:hl, :] = jnp.zeros((hl, d), F32)

    @pl.when(s > 0)
    def _():
        abuf[0:hl, :] = abuf[ts:ts + hl, :]

    shift = ada_ref[0, 0, 0:1, :]
    scale = ada_ref[0, 0, 1:2, :]
    h = (_rms(x_ref[0], ng_ref[...] * (1.0 + scale)) + shift).astype(BF16)

    a = _dot(h, win_ref[:, 0:d])
    abuf[hl:hl + ts, :] = a
    u = _dot(h, win_ref[:, d:2 * d])

    n2 = ts + hl - 8
    s2buf[8:8 + n2, :] = abuf[8:8 + n2, :] + abuf[7:7 + n2, :]
    n4 = ts + hl - 16
    s4buf[16:16 + n4, :] = s2buf[16:16 + n4, g:] + s2buf[14:14 + n4, g:]
    n8 = ts + hl - 24
    s8buf[24:24 + n8, :] = s4buf[24:24 + n8, g:] + s4buf[20:20 + n8, g:]
    s16buf[hl:hl + ts, :] = s8buf[hl:hl + ts, g:] + s8buf[hl - 8:hl - 8 + ts, g:]
    v = _dot(h, win_ref[:, 2 * d:3 * d])

    t_glob = s * ts + lax.broadcasted_iota(jnp.int32, (ts, 1), 0)
    win_sums = (s2buf[hl:hl + ts, 0:g], s4buf[hl:hl + ts, 0:g], s8buf[hl:hl + ts, 0:g], s16buf[hl:hl + ts, :])
    ya = []
    for gi, w in enumerate(POOL_WINDOWS):
        count = jnp.minimum(t_glob + 1, w).astype(F32)
        p = win_sums[gi] / count - a[:, gi * g:(gi + 1) * g]
        ya.append(_dot(p.astype(BF16), pw_ref[gi]))
    gate_a = _silu(_dot(h, win_ref[:, 3 * d:4 * d]))
    ya = jnp.concatenate(ya, axis=1) * ps_ref[...] * gate_a

    vn = _rms(v, sg_ref[...]).astype(BF16)
    gate_b = _silu(_dot(h, win_ref[:, 4 * d:5 * d]))
    nblk = ts // SGU_LEN
    row = lax.broadcasted_iota(jnp.int32, (SGU_LEN, SGU_LEN), 0)
    col = lax.broadcasted_iota(jnp.int32, (SGU_LEN, SGU_LEN), 1)
    hd = d // SGU_HEADS
    z_cols = []
    for hh in range(SGU_HEADS):
        wm = jnp.where(col <= row, sw_ref[hh], 0.0).astype(BF16)
        vh = jnp.concatenate(
            [vn[n * SGU_LEN:(n + 1) * SGU_LEN, hh * hd:(hh + 1) * hd] for n in range(nblk)], axis=1)
        zh = _dot(wm, vh) + sbt_ref[:, hh:hh + 1]
        z_cols.append(jnp.concatenate([zh[:, n * hd:(n + 1) * hd] for n in range(nblk)], axis=0))
    yb = u * jnp.concatenate(z_cols, axis=1) * gate_b

    y = jnp.concatenate([ya, yb], axis=1).astype(BF16)
    o_ref[0] = x_ref[0] + ada_ref[0, 0, 2:3, :] * _dot(y, wout_ref[...])


def _even_call(x, ada4, layer, norm_g, in_w, pool_w, pool_scale, sgu_norm_g, sgu_w, sgu_bt, out_w, riders):
    b, sq, d = x.shape
    ts = TS_EVEN
    g = POOL_GROUP
    rows = ts + POOL_HALO
    nj = sq // ts
    rider_blocks = [_rider_blocks(r.shape[0], b * nj) for r in riders]

    def rider_spec(r, br, nblk):
        return pl.BlockSpec((br, r.shape[1]), lambda i, j: (jnp.minimum(i * nj + j, nblk - 1), 0))

    rider_specs = [rider_spec(r, br, nblk) for r, (br, nblk) in zip(riders, rider_blocks)]
    return pl.pallas_call(
        functools.partial(_even_kernel, ts=ts, rider_rows=tuple(r.shape[0] for r in riders)),
        grid=(b, nj),
        in_specs=[
            pl.BlockSpec((1, ts, d), lambda i, j: (i, j, 0)),
            pl.BlockSpec((1, 1, 3, d), lambda i, j: (layer, i, 0, 0)),
            _resident(norm_g.shape),
            _resident(in_w.shape),
            _resident(pool_w.shape),
            _resident(pool_scale.shape),
            _resident(sgu_norm_g.shape),
            _resident(sgu_w.shape),
            _resident(sgu_bt.shape),
            _resident(out_w.shape),
        ] + rider_specs,
        out_specs=[pl.BlockSpec((1, ts, d), lambda i, j: (i, j, 0))] + rider_specs,
        out_shape=[jax.ShapeDtypeStruct(x.shape, F32)] + [
            jax.ShapeDtypeStruct((br * nblk, r.shape[1]), BF16) for r, (br, nblk) in zip(riders, rider_blocks)],
        scratch_shapes=[
            pltpu.VMEM((rows, d), F32),
            pltpu.VMEM((rows, d), F32),
            pltpu.VMEM((rows, d - g), F32),
            pltpu.VMEM((rows, d - 2 * g), F32),
            pltpu.VMEM((rows, d - 3 * g), F32),
        ],
        compiler_params=pltpu.CompilerParams(
            dimension_semantics=("arbitrary", "arbitrary"), vmem_limit_bytes=VMEM_LIMIT),
        name="even_layer",
    )(x, ada4, norm_g, in_w, pool_w, pool_scale, sgu_norm_g, sgu_w, sgu_bt, out_w, *riders)


def _odd_kernel(x_ref, ada_ref, ng_ref, wt_ref, gw_ref, gb_ref, gng_ref, wout_ref, fg_ref,
                o_ref, state, kbuf, vtbuf, qtbuf, acc, crow, ybuf, gbuf, *, ts):
    s = pl.program_id(1)

    @pl.when(s == 0)
    def _():
        state[...] = jnp.zeros(state.shape, F32)

    shift = ada_ref[0, 0, 0:1, :]
    scale = ada_ref[0, 0, 1:2, :]
    h = (_rms(x_ref[0], ng_ref[...] * (1.0 + scale)) + shift).astype(BF16)

    kw = GLA_HEADS * GLA_DK
    vw = GLA_HEADS * GLA_DV
    sbw = SB_HEADS * SB_DH
    r_gq, r_gk, r_gv = 0, kw, 2 * kw
    r_glr = r_gv + vw
    r_sq = r_glr + GLA_RANK
    r_sk, r_sv, r_gate = r_sq + sbw, r_sq + 2 * sbw, r_sq + 3 * sbw

    def proj(r0, n):
        return _dot_nt(h, wt_ref[r0:r0 + n, :])

    sb_scale = SB_DH ** -0.5
    s0 = pl.multiple_of(s * ts, ts)

    sq = proj(r_sq, sbw) * sb_scale
    sk = proj(r_sk, sbw).astype(BF16)
    sv = proj(r_sv, sbw)
    for hh in range(SB_HEADS):
        hs = slice(hh * SB_DH, (hh + 1) * SB_DH)
        qtbuf[hh] = sq[:, hs].T.astype(BF16)
        kbuf[hh, pl.ds(s0, ts), :] = sk[:, hs]
        vtbuf[hh, s] = sv[:, hs].T.astype(BF16)

    nchunk = ts // GLA_CHUNK
    heads = range(GLA_HEADS)
    kss = [slice(hh * GLA_DK, (hh + 1) * GLA_DK) for hh in heads]
    vss = [slice(hh * GLA_DV, (hh + 1) * GLA_DV) for hh in heads]
    r = lax.broadcasted_iota(jnp.int32, (ts, ts), 0)
    cc = lax.broadcasted_iota(jnp.int32, (ts, ts), 1)
    same_chunk_causal = (cc <= r) & (cc // GLA_CHUNK == r // GLA_CHUNK)
    gla = {}

    def gla_decay_input():
        glr = proj(r_glr, GLA_RANK)
        gla["gq"] = proj(r_gq, kw) * (GLA_DK ** -0.5)
        glr_hi, glr_lo = _split2(glr)
        gw_hi, gw_lo = _split2(gw_ref[...])
        pre = _dot(jnp.concatenate([glr_hi, glr_lo, glr_hi], axis=1),
                   jnp.concatenate([gw_hi, gw_hi, gw_lo], axis=0)) + gb_ref[...]
        gla["la"] = -_softplus(-pre) * (1.0 / GLA_TAU)

    def gla_decay_sums():
        gla["gk"] = proj(r_gk, kw)
        tri = jnp.where(same_chunk_causal, 1.0, 0.0).astype(BF16)
        la_hi, la_lo = _split2(gla["la"])
        gla["bcum"] = _dot(jnp.concatenate([tri, tri], axis=1), jnp.concatenate([la_hi, la_lo], axis=0))
        gla["gv"] = proj(r_gv, vw).astype(BF16)

    def gla_scores():
        gq, gk, bcum, gv_b = gla["gq"], gla["gk"], gla["bcum"], gla["gv"]
        gla["q_dec"] = q_dec = (gq * jnp.exp(bcum)).astype(BF16)
        k_inv = (gk * jnp.exp(-bcum)).astype(BF16)
        b_last = jnp.concatenate(
            [jnp.broadcast_to(bcum[(ci + 1) * GLA_CHUNK - 1:(ci + 1) * GLA_CHUNK, :], (GLA_CHUNK, kw))
             for ci in range(nchunk)], axis=0)
        k_end = gk * jnp.exp(b_last - bcum)
        gla["zs"] = [_dot_nt(q_dec[:, kss[hh]], k_inv[:, kss[hh]]) for hh in heads]
        lane_chunk = lax.broadcasted_iota(jnp.int32, (GLA_DK, ts), 1) // GLA_CHUNK
        gla["us"], gla["bcts"] = [], []
        for hh in heads:
            ket = k_end[:, kss[hh]].T
            gla["bcts"].append(bcum[:, kss[hh]].T)
            lhs = jnp.concatenate(
                [jnp.where(lane_chunk == ci, ket, 0.0).astype(BF16) for ci in range(nchunk)], axis=0)
            gla["us"].append(_dot(lhs, gv_b[:, vss[hh]]))

    def gla_outputs():
        for hh in heads:
            att = jnp.where(same_chunk_causal, gla["zs"][hh], 0.0).astype(BF16)
            st = state[hh]
            starts = []
            for ci in range(nchunk):
                starts.append(st.astype(BF16))
                last = (ci + 1) * GLA_CHUNK - 1
                st = (jnp.exp(gla["bcts"][hh][:, last:last + 1]) * st
                      + gla["us"][hh][ci * GLA_DK:(ci + 1) * GLA_DK, :])
            state[hh] = st
            qh = gla["q_dec"][:, kss[hh]]
            o = _dot(att, gla["gv"][:, vss[hh]]) + jnp.concatenate(
                [_dot(qh[ci * GLA_CHUNK:(ci + 1) * GLA_CHUNK, :], starts[ci]) for ci in range(nchunk)], axis=0)
            ybuf[:, vss[hh]] = _rms(o, gng_ref[:, vss[hh]])

    groups = [tuple(range(g, g + SB_GROUP)) for g in range(0, SB_HEADS, SB_GROUP)]
    n_gate = SB_STAGES * len(groups) - 4
    gate_chunk = (vw + sbw) // n_gate

    def gate_filler(ci):
        def run():
            lo = ci * gate_chunk
            gbuf[:, lo:lo + gate_chunk] = _silu(proj(r_gate + lo, gate_chunk))
        return run

    fillers = [gla_decay_input, gla_decay_sums, gla_scores, gla_outputs] + [gate_filler(ci) for ci in range(n_gate)]

    jr = lax.broadcasted_iota(jnp.int32, (ts, ts), 0)
    jc = lax.broadcasted_iota(jnp.int32, (ts, ts), 1)
    later = jnp.where(jc > jr, 1.0, 0.0).astype(BF16)
    valid = jr < jc

    def sb_blocks(heads, kbs, fresh, penalty=None, between=()):
        between = list(between)

        def stage_done():
            if between:
                between.pop(0)()

        zs = [[_dot(kbuf[hh, pl.ds(pl.multiple_of(kb * ts, ts), ts), :], qtbuf[hh]) for hh in heads] for kb in kbs]
        stage_done()
        sps, lbetas, firsts = [], [], []
        for bi, zb in enumerate(zs):
            sps.append([])
            lbetas.append([])
            firsts.append([])
            for z in zb:
                lg = jnp.log(1.0 + jnp.exp2(jnp.abs(z) * (-LOG2E)))
                sp = jnp.maximum(z, 0.0) + lg
                if fresh and bi == 0:
                    sp = jnp.where(valid, sp, 0.0)
                sps[bi].append(sp.astype(BF16))
                firsts[bi].append(sp[0:1, :])
                lbetas[bi].append(jnp.minimum(z, 0.0) - lg)
        stage_done()
        sums = [[_dot(later, sp) for sp in spb] for spb in sps]
        stage_done()
        ws, carries = [], []
        for hi, hh in enumerate(heads):
            carry = None if fresh else crow[hh]
            ws.append([])
            for bi in range(len(kbs)):
                xarg = lbetas[bi][hi] - sums[bi][hi]
                if carry is not None:
                    xarg = xarg - carry
                w = jnp.exp(xarg)
                if fresh and bi == 0:
                    w = jnp.where(valid, w, 0.0)
                ws[hi].append(w.astype(BF16))
                total = sums[bi][hi][0:1, :] + firsts[bi][hi]
                carry = total if carry is None else carry + total
                if penalty is not None and bi == 0:
                    carry = carry + penalty
            carries.append(carry)
        stage_done()
        for hi, hh in enumerate(heads):
            contrib = _dot(jnp.concatenate([vtbuf[hh, kb] for kb in kbs], axis=1),
                           jnp.concatenate(ws[hi], axis=0))
            acc[hh] = contrib if fresh else acc[hh] + contrib
            crow[hh] = carries[hi]

    def any_alive():
        low = crow[0]
        for hh in range(1, SB_HEADS):
            low = jnp.minimum(low, crow[hh])
        return (jnp.min(low) < SB_DEAD_CARRY).astype(jnp.int32)

    penalty = jnp.where(s > 0, 0.0, SB_NO_BLOCK)
    for gi, heads in enumerate(groups):
        sb_blocks(heads, (s, jnp.maximum(s - 1, 0)), True, penalty, fillers[gi * SB_STAGES:(gi + 1) * SB_STAGES])

    def more_blocks(state_):
        i, alive = state_
        return jnp.logical_and(i < s, alive > 0)

    def key_block(state_):
        i, _ = state_
        for heads in groups:
            sb_blocks(heads, (s - 1 - i,), False)
        return i + 1, any_alive()

    lax.while_loop(more_blocks, key_block, (jnp.int32(1), any_alive()))

    for hh in range(SB_HEADS):
        ybuf[:, vw + hh * SB_DH:vw + (hh + 1) * SB_DH] = acc[hh].T

    y = (ybuf[...] * gbuf[...]).astype(BF16)
    x1 = x_ref[0] + ada_ref[0, 0, 2:3, :] * _dot(y, wout_ref[...])
    o_ref[0] = _rms(x1, fg_ref[...])


def _odd_call(x, ada4, layer, norm_g, in_wt, gate_w, gate_b, gla_norm_g, out_w, final_g):
    b, sq, d = x.shape
    ts = TS_ODD
    return pl.pallas_call(
        functools.partial(_odd_kernel, ts=ts),
        grid=(b, sq // ts),
        in_specs=[
            pl.BlockSpec((1, ts, d), lambda i, j: (i, j, 0)),
            pl.BlockSpec((1, 1, 3, d), lambda i, j: (layer, i, 0, 0)),
            _resident(norm_g.shape),
            _resident(in_wt.shape),
            _resident(gate_w.shape),
            _resident(gate_b.shape),
            _resident(gla_norm_g.shape),
            _resident(out_w.shape),
            _resident(final_g.shape),
        ],
        out_specs=pl.BlockSpec((1, ts, d), lambda i, j: (i, j, 0)),
        out_shape=jax.ShapeDtypeStruct(x.shape, F32),
        scratch_shapes=[
            pltpu.VMEM((GLA_HEADS, GLA_DK, GLA_DV), F32),
            pltpu.VMEM((SB_HEADS, sq, SB_DH), BF16),
            pltpu.VMEM((SB_HEADS, sq // ts, SB_DH, ts), BF16),
            pltpu.VMEM((SB_HEADS, SB_DH, ts), BF16),
            pltpu.VMEM((SB_HEADS, SB_DH, ts), F32),
            pltpu.VMEM((SB_HEADS, 1, ts), F32),
            pltpu.VMEM((ts, 2 * d), F32),
            pltpu.VMEM((ts, 2 * d), F32),
        ],
        compiler_params=pltpu.CompilerParams(
            dimension_semantics=("arbitrary", "arbitrary"), vmem_limit_bytes=VMEM_LIMIT),
        name="odd_layer",
    )(x, ada4, norm_g, in_wt, gate_w, gate_b, gla_norm_g, out_w, final_g)


def kernel(x, c, ada_w, ada_b, norm_g, even_in_w, pool_w, pool_scale, sgu_norm_g, sgu_w, sgu_b, even_out_w,
           odd_in_w, gla_gate_w, gla_gate_b, gla_norm_g, odd_out_w, final_g):
    depth, d, _ = ada_w.shape
    assert depth == 2 and even_in_w.shape[0] == 1 and odd_in_w.shape[0] == 1
    b = x.shape[0]
    ada4 = _ada_call(c, ada_w, ada_b).reshape(depth, b, 3, d)

    x, in_wt, odd_out = _even_call(
        x, ada4, 0, norm_g[0:1], even_in_w[0].astype(BF16), pool_w[0].astype(BF16), pool_scale[0:1],
        sgu_norm_g[0:1], sgu_w[0], sgu_b[0].T, even_out_w[0].astype(BF16),
        [jnp.swapaxes(odd_in_w[0], 0, 1), odd_out_w[0]])
    return _odd_call(
        x, ada4, 1, norm_g[1:2], in_wt, gla_gate_w[0], gla_gate_b[0:1], gla_norm_g[0].reshape(1, -1),
        odd_out, final_g.reshape(1, -1))
```

```python
import functools

import jax
import jax.numpy as jnp
from jax import lax
from jax.experimental import pallas as pl
from jax.experimental.pallas import tpu as pltpu

F32 = jnp.float32
BF16 = jnp.bfloat16

EPS = 1e-6
LOG2E = 1.4426950408889634
POOL_WINDOWS = (2, 4, 8, 16)
POOL_GROUP = 256
POOL_HALO = 32
SGU_LEN = 128
SGU_HEADS = 8
GLA_HEADS = 4
GLA_DK = 128
GLA_DV = 256
GLA_CHUNK = 64
GLA_RANK = 16
GLA_TAU = 16.0
SB_HEADS = 8
SB_DH = 128
SB_GROUP = 4
SB_STAGES = 4
SB_DEAD_CARRY = 105.0
SB_NO_BLOCK = 1.0e4

TS_EVEN = 512
TS_ODD = 256
VMEM_LIMIT = 56 * 1024 * 1024


def _dot(a, b):
    return jnp.dot(a, b, preferred_element_type=F32)


def _dot_nt(a, b):
    return lax.dot_general(a, b, (((1,), (1,)), ((), ())), preferred_element_type=F32)


def _split2(x):
    hi = x.astype(BF16)
    lo = (x - hi.astype(F32)).astype(BF16)
    return hi, lo


def _sigmoid(x):
    return 1.0 / (1.0 + jnp.exp(-x))


def _silu(x):
    return x * _sigmoid(x)


def _softplus(x):
    return jnp.maximum(x, 0.0) + jnp.log(1.0 + jnp.exp(-jnp.abs(x)))


def _rms(x, g):
    return x * lax.rsqrt(jnp.mean(x * x, axis=-1, keepdims=True) + EPS) * g


def _resident(shape):
    n = len(shape)
    return pl.BlockSpec(shape, lambda *_: (0,) * n, pipeline_mode=pl.Buffered(1))


def _rider_blocks(rows, steps):
    br = -(-rows // steps)
    br = -(-br // 16) * 16
    return br, -(-rows // br)


def _ada_kernel(c_ref, w_ref, b_ref, o_ref):
    sc = _silu(c_ref[...])
    o_ref[0] = jnp.dot(sc, w_ref[0], preferred_element_type=F32,
                       precision=lax.Precision.HIGHEST) + b_ref[0]


def _ada_call(c, ada_w, ada_b):
    depth, d, d3 = ada_w.shape
    b = c.shape[0]
    tn = 1024
    return pl.pallas_call(
        _ada_kernel,
        grid=(depth, d3 // tn),
        in_specs=[
            pl.BlockSpec((b, d), lambda i, j: (0, 0)),
            pl.BlockSpec((1, d, tn), lambda i, j: (i, 0, j)),
            pl.BlockSpec((1, 1, tn), lambda i, j: (i, 0, j)),
        ],
        out_specs=pl.BlockSpec((1, b, tn), lambda i, j: (i, 0, j)),
        out_shape=jax.ShapeDtypeStruct((depth, b, d3), F32),
        compiler_params=pltpu.CompilerParams(
            dimension_semantics=("arbitrary", "arbitrary"), vmem_limit_bytes=VMEM_LIMIT),
        name="ada",
    )(c, ada_w, ada_b.reshape(depth, 1, d3))


def _even_kernel(x_ref, ada_ref, ng_ref, win_ref, pw_ref, ps_ref, sg_ref, sw_ref, sbt_ref, wout_ref,
                 *rest, ts, rider_rows):
    n_rider = len(rider_rows)
    rider_in, o_ref, rider_out = rest[:n_rider], rest[n_rider], rest[n_rider + 1:2 * n_rider + 1]
    abuf, s2buf, s4buf, s8buf, s16buf = rest[2 * n_rider + 1:]
    s = pl.program_id(1)

    step = pl.program_id(0) * pl.num_programs(1) + s
    for src, dst, rows in zip(rider_in, rider_out, rider_rows):
        br = src.shape[0]
        blk = jnp.minimum(step, -(-rows // br) - 1)
        row = blk * br + lax.broadcasted_iota(jnp.int32, src.shape, 0)
        dst[...] = jnp.where(row < rows, src[...], 0.0).astype(dst.dtype)

    d = x_ref.shape[-1]
    hl = POOL_HALO
    g = POOL_GROUP

    @pl.when(s == 0)
    def _():
        abuf[0:hl, :] = jnp.zeros((hl, d), F32)

    @pl.when(s > 0)
    def _():
        abuf[0:hl, :] = abuf[ts:ts + hl, :]

    shift = ada_ref[0, 0, 0:1, :]
    scale = ada_ref[0, 0, 1:2, :]
    h = (_rms(x_ref[0], ng_ref[...] * (1.0 + scale)) + shift).astype(BF16)

    a = _dot(h, win_ref[:, 0:d])
    abuf[hl:hl + ts, :] = a
    u = _dot(h, win_ref[:, d:2 * d])

    n2 = ts + hl - 8
    s2buf[8:8 + n2, :] = abuf[8:8 + n2, :] + abuf[7:7 + n2, :]
    n4 = ts + hl - 16
    s4buf[16:16 + n4, :] = s2buf[16:16 + n4, g:] + s2buf[14:14 + n4, g:]
    n8 = ts + hl - 24
    s8buf[24:24 + n8, :] = s4buf[24:24 + n8, g:] + s4buf[20:20 + n8, g:]
    s16buf[hl:hl + ts, :] = s8buf[hl:hl + ts, g:] + s8buf[hl - 8:hl - 8 + ts, g:]
    v = _dot(h, win_ref[:, 2 * d:3 * d])

    t_glob = s * ts + lax.broadcasted_iota(jnp.int32, (ts, 1), 0)
    win_sums = (s2buf[hl:hl + ts, 0:g], s4buf[hl:hl + ts, 0:g], s8buf[hl:hl + ts, 0:g], s16buf[hl:hl + ts, :])
    ya = []
    for gi, w in enumerate(POOL_WINDOWS):
        count = jnp.minimum(t_glob + 1, w).astype(F32)
        p = win_sums[gi] / count - a[:, gi * g:(gi + 1) * g]
        ya.append(_dot(p.astype(BF16), pw_ref[gi]))
    gate_a = _silu(_dot(h, win_ref[:, 3 * d:4 * d]))
    ya = jnp.concatenate(ya, axis=1) * ps_ref[...] * gate_a

    vn = _rms(v, sg_ref[...]).astype(BF16)
    gate_b = _silu(_dot(h, win_ref[:, 4 * d:5 * d]))
    nblk = ts // SGU_LEN
    row = lax.broadcasted_iota(jnp.int32, (SGU_LEN, SGU_LEN), 0)
    col = lax.broadcasted_iota(jnp.int32, (SGU_LEN, SGU_LEN), 1)
    hd = d // SGU_HEADS
    z_cols = []
    for hh in range(SGU_HEADS):
        wm = jnp.where(col <= row, sw_ref[hh], 0.0).astype(BF16)
        vh = jnp.concatenate(
            [vn[n * SGU_LEN:(n + 1) * SGU_LEN, hh * hd:(hh + 1) * hd] for n in range(nblk)], axis=1)
        zh = _dot(wm, vh) + sbt_ref[:, hh:hh + 1]
        z_cols.append(jnp.concatenate([zh[:, n * hd:(n + 1) * hd] for n in range(nblk)], axis=0))
    yb = u * jnp.concatenate(z_cols, axis=1) * gate_b

    y = jnp.concatenate([ya, yb], axis=1).astype(BF16)
    o_ref[0] = x_ref[0] + ada_ref[0, 0, 2:3, :] * _dot(y, wout_ref[...])


def _even_call(x, ada4, layer, norm_g, in_w, pool_w, pool_scale, sgu_norm_g, sgu_w, sgu_bt, out_w, riders):
    b, sq, d = x.shape
    ts = TS_EVEN
    g = POOL_GROUP
    rows = ts + POOL_HALO
    nj = sq // ts
    rider_blocks = [_rider_blocks(r.shape[0], b * nj) for r in riders]

    def rider_spec(r, br, nblk):
        return pl.BlockSpec((br, r.shape[1]), lambda i, j: (jnp.minimum(i * nj + j, nblk - 1), 0))

    rider_specs = [rider_spec(r, br, nblk) for r, (br, nblk) in zip(riders, rider_blocks)]
    return pl.pallas_call(
        functools.partial(_even_kernel, ts=ts, rider_rows=tuple(r.shape[0] for r in riders)),
        grid=(b, nj),
        in_specs=[
            pl.BlockSpec((1, ts, d), lambda i, j: (i, j, 0)),
            pl.BlockSpec((1, 1, 3, d), lambda i, j: (layer, i, 0, 0)),
            _resident(norm_g.shape),
            _resident(in_w.shape),
            _resident(pool_w.shape),
            _resident(pool_scale.shape),
            _resident(sgu_norm_g.shape),
            _resident(sgu_w.shape),
            _resident(sgu_bt.shape),
            _resident(out_w.shape),
        ] + rider_specs,
        out_specs=[pl.BlockSpec((1, ts, d), lambda i, j: (i, j, 0))] + rider_specs,
        out_shape=[jax.ShapeDtypeStruct(x.shape, F32)] + [
            jax.ShapeDtypeStruct((br * nblk, r.shape[1]), BF16) for r, (br, nblk) in zip(riders, rider_blocks)],
        scratch_shapes=[
            pltpu.VMEM((rows, d), F32),
            pltpu.VMEM((rows, d), F32),
            pltpu.VMEM((rows, d - g), F32),
            pltpu.VMEM((rows, d - 2 * g), F32),
            pltpu.VMEM((rows, d - 3 * g), F32),
        ],
        compiler_params=pltpu.CompilerParams(
            dimension_semantics=("arbitrary", "arbitrary"), vmem_limit_bytes=VMEM_LIMIT),
        name="even_layer",
    )(x, ada4, norm_g, in_w, pool_w, pool_scale, sgu_norm_g, sgu_w, sgu_bt, out_w, *riders)


def _odd_kernel(x_ref, ada_ref, ng_ref, wt_ref, gw_ref, gb_ref, gng_ref, wout_ref, fg_ref,
                o_ref, state, kbuf, vtbuf, qtbuf, acc, crow, ybuf, gbuf, *, ts):
    s = pl.program_id(1)

    @pl.when(s == 0)
    def _():
        state[...] = jnp.zeros(state.shape, F32)

    shift = ada_ref[0, 0, 0:1, :]
    scale = ada_ref[0, 0, 1:2, :]
    h = (_rms(x_ref[0], ng_ref[...] * (1.0 + scale)) + shift).astype(BF16)

    kw = GLA_HEADS * GLA_DK
    vw = GLA_HEADS * GLA_DV
    sbw = SB_HEADS * SB_DH
    r_gq, r_gk, r_gv = 0, kw, 2 * kw
    r_glr = r_gv + vw
    r_sq = r_glr + GLA_RANK
    r_sk, r_sv, r_gate = r_sq + sbw, r_sq + 2 * sbw, r_sq + 3 * sbw

    def proj(r0, n):
        return _dot_nt(h, wt_ref[r0:r0 + n, :])

    sb_scale = SB_DH ** -0.5
    s0 = pl.multiple_of(s * ts, ts)

    sq = proj(r_sq, sbw) * sb_scale
    sk = proj(r_sk, sbw).astype(BF16)
    sv = proj(r_sv, sbw)
    for hh in range(SB_HEADS):
        hs = slice(hh * SB_DH, (hh + 1) * SB_DH)
        qtbuf[hh] = sq[:, hs].T.astype(BF16)
        kbuf[hh, pl.ds(s0, ts), :] = sk[:, hs]
        vtbuf[hh, s] = sv[:, hs].T.astype(BF16)

    nchunk = ts // GLA_CHUNK
    heads = range(GLA_HEADS)
    kss = [slice(hh * GLA_DK, (hh + 1) * GLA_DK) for hh in heads]
    vss = [slice(hh * GLA_DV, (hh + 1) * GLA_DV) for hh in heads]
    r = lax.broadcasted_iota(jnp.int32, (ts, ts), 0)
    cc = lax.broadcasted_iota(jnp.int32, (ts, ts), 1)
    same_chunk_causal = (cc <= r) & (cc // GLA_CHUNK == r // GLA_CHUNK)
    gla = {}

    def gla_decay_input():
        glr = proj(r_glr, GLA_RANK)
        gla["gq"] = proj(r_gq, kw) * (GLA_DK ** -0.5)
        glr_hi, glr_lo = _split2(glr)
        gw_hi, gw_lo = _split2(gw_ref[...])
        pre = _dot(jnp.concatenate([glr_hi, glr_lo, glr_hi], axis=1),
                   jnp.concatenate([gw_hi, gw_hi, gw_lo], axis=0)) + gb_ref[...]
        gla["la"] = -_softplus(-pre) * (1.0 / GLA_TAU)

    def gla_decay_sums():
        gla["gk"] = proj(r_gk, kw)
        tri = jnp.where(same_chunk_causal, 1.0, 0.0).astype(BF16)
        la_hi, la_lo = _split2(gla["la"])
        gla["bcum"] = _dot(jnp.concatenate([tri, tri], axis=1), jnp.concatenate([la_hi, la_lo], axis=0))
        gla["gv"] = proj(r_gv, vw).astype(BF16)

    def gla_scores():
        gq, gk, bcum, gv_b = gla["gq"], gla["gk"], gla["bcum"], gla["gv"]
        gla["q_dec"] = q_dec = (gq * jnp.exp(bcum)).astype(BF16)
        k_inv = (gk * jnp.exp(-bcum)).astype(BF16)
        b_last = jnp.concatenate(
            [jnp.broadcast_to(bcum[(ci + 1) * GLA_CHUNK - 1:(ci + 1) * GLA_CHUNK, :], (GLA_CHUNK, kw))
             for ci in range(nchunk)], axis=0)
        k_end = gk * jnp.exp(b_last - bcum)
        gla["zs"] = [_dot_nt(q_dec[:, kss[hh]], k_inv[:, kss[hh]]) for hh in heads]
        lane_chunk = lax.broadcasted_iota(jnp.int32, (GLA_DK, ts), 1) // GLA_CHUNK
        gla["us"], gla["bcts"] = [], []
        for hh in heads:
            ket = k_end[:, kss[hh]].T
            gla["bcts"].append(bcum[:, kss[hh]].T)
            lhs = jnp.concatenate(
                [jnp.where(lane_chunk == ci, ket, 0.0).astype(BF16) for ci in range(nchunk)], axis=0)
            gla["us"].append(_dot(lhs, gv_b[:, vss[hh]]))

    def gla_outputs():
        for hh in heads:
            att = jnp.where(same_chunk_causal, gla["zs"][hh], 0.0).astype(BF16)
            st = state[hh]
            starts = []
            for ci in range(nchunk):
                starts.append(st.astype(BF16))
                last = (ci + 1) * GLA_CHUNK - 1
                st = (jnp.exp(gla["bcts"][hh][:, last:last + 1]) * st
                      + gla["us"][hh][ci * GLA_DK:(ci + 1) * GLA_DK, :])
            state[hh] = st
            qh = gla["q_dec"][:, kss[hh]]
            o = _dot(att, gla["gv"][:, vss[hh]]) + jnp.concatenate(
                [_dot(qh[ci * GLA_CHUNK:(ci + 1) * GLA_CHUNK, :], starts[ci]) for ci in range(nchunk)], axis=0)
            ybuf[:, vss[hh]] = _rms(o, gng_ref[:, vss[hh]])

    groups = [tuple(range(g, g + SB_GROUP)) for g in range(0, SB_HEADS, SB_GROUP)]
    n_gate = SB_STAGES * len(groups) - 4
    assert n_gate > 0 and (vw + sbw) % n_gate == 0
    gate_chunk = (vw + sbw) // n_gate

    def gate_filler(ci):
        def run():
            lo = ci * gate_chunk
            gbuf[:, lo:lo + gate_chunk] = _silu(proj(r_gate + lo, gate_chunk))
        return run

    fillers = [gla_decay_input, gla_decay_sums, gla_scores, gla_outputs] + [gate_filler(ci) for ci in range(n_gate)]
    assert len(fillers) == SB_STAGES * len(groups)

    jr = lax.broadcasted_iota(jnp.int32, (ts, ts), 0)
    jc = lax.broadcasted_iota(jnp.int32, (ts, ts), 1)
    later = jnp.where(jc > jr, 1.0, 0.0).astype(BF16)
    valid = jr < jc

    def sb_blocks(heads, kbs, fresh, penalty=None, between=()):
        between = list(between)

        def stage_done():
            if between:
                between.pop(0)()

        zs = [[_dot(kbuf[hh, pl.ds(pl.multiple_of(kb * ts, ts), ts), :], qtbuf[hh]) for hh in heads] for kb in kbs]
        stage_done()
        sps, lbetas, firsts = [], [], []
        for bi, zb in enumerate(zs):
            sps.append([])
            lbetas.append([])
            firsts.append([])
            for z in zb:
                lg = jnp.log(1.0 + jnp.exp2(jnp.abs(z) * (-LOG2E)))
                sp = jnp.maximum(z, 0.0) + lg
                if fresh and bi == 0:
                    sp = jnp.where(valid, sp, 0.0)
                sps[bi].append(sp.astype(BF16))
                firsts[bi].append(sp[0:1, :])
                lbetas[bi].append(jnp.minimum(z, 0.0) - lg)
        stage_done()
        sums = [[_dot(later, sp) for sp in spb] for spb in sps]
        stage_done()
        ws, carries = [], []
        for hi, hh in enumerate(heads):
            carry = None if fresh else crow[hh]
            ws.append([])
            for bi in range(len(kbs)):
                xarg = lbetas[bi][hi] - sums[bi][hi]
                if carry is not None:
                    xarg = xarg - carry
                w = jnp.exp(xarg)
                if fresh and bi == 0:
                    w = jnp.where(valid, w, 0.0)
                ws[hi].append(w.astype(BF16))
                total = sums[bi][hi][0:1, :] + firsts[bi][hi]
                carry = total if carry is None else carry + total
                if penalty is not None and bi == 0:
                    carry = carry + penalty
            carries.append(carry)
        stage_done()
        for hi, hh in enumerate(heads):
            contrib = _dot(jnp.concatenate([vtbuf[hh, kb] for kb in kbs], axis=1),
                           jnp.concatenate(ws[hi], axis=0))
            acc[hh] = contrib if fresh else acc[hh] + contrib
            crow[hh] = carries[hi]

    def any_alive():
        low = crow[0]
        for hh in range(1, SB_HEADS):
            low = jnp.minimum(low, crow[hh])
        return (jnp.min(low) < SB_DEAD_CARRY).astype(jnp.int32)

    penalty = jnp.where(s > 0, 0.0, SB_NO_BLOCK)
    for gi, heads in enumerate(groups):
        sb_blocks(heads, (s, jnp.maximum(s - 1, 0)), True, penalty, fillers[gi * SB_STAGES:(gi + 1) * SB_STAGES])

    def more_blocks(state_):
        i, alive = state_
        return jnp.logical_and(i < s, alive > 0)

    def key_block(state_):
        i, _ = state_
        for heads in groups:
            sb_blocks(heads, (s - 1 - i,), False)
        return i + 1, any_alive()

    lax.while_loop(more_blocks, key_block, (jnp.int32(1), any_alive()))

    for hh in range(SB_HEADS):
        ybuf[:, vw + hh * SB_DH:vw + (hh + 1) * SB_DH] = acc[hh].T

    y = (ybuf[...] * gbuf[...]).astype(BF16)
    x1 = x_ref[0] + ada_ref[0, 0, 2:3, :] * _dot(y, wout_ref[...])
    o_ref[0] = _rms(x1, fg_ref[...])


def _odd_call(x, ada4, layer, norm_g, in_wt, gate_w, gate_b, gla_norm_g, out_w, final_g):
    b, sq, d = x.shape
    ts = TS_ODD
    return pl.pallas_call(
        functools.partial(_odd_kernel, ts=ts),
        grid=(b, sq // ts),
        in_specs=[
            pl.BlockSpec((1, ts, d), lambda i, j: (i, j, 0)),
            pl.BlockSpec((1, 1, 3, d), lambda i, j: (layer, i, 0, 0)),
            _resident(norm_g.shape),
            _resident(in_wt.shape),
            _resident(gate_w.shape),
            _resident(gate_b.shape),
            _resident(gla_norm_g.shape),
            _resident(out_w.shape),
            _resident(final_g.shape),
        ],
        out_specs=pl.BlockSpec((1, ts, d), lambda i, j: (i, j, 0)),
        out_shape=jax.ShapeDtypeStruct(x.shape, F32),
        scratch_shapes=[
            pltpu.VMEM((GLA_HEADS, GLA_DK, GLA_DV), F32),
            pltpu.VMEM((SB_HEADS, sq, SB_DH), BF16),
            pltpu.VMEM((SB_HEADS, sq // ts, SB_DH, ts), BF16),
            pltpu.VMEM((SB_HEADS, SB_DH, ts), BF16),
            pltpu.VMEM((SB_HEADS, SB_DH, ts), F32),
            pltpu.VMEM((SB_HEADS, 1, ts), F32),
            pltpu.VMEM((ts, 2 * d), F32),
            pltpu.VMEM((ts, 2 * d), F32),
        ],
        compiler_params=pltpu.CompilerParams(
            dimension_semantics=("arbitrary", "arbitrary"), vmem_limit_bytes=VMEM_LIMIT),
        name="odd_layer",
    )(x, ada4, norm_g, in_wt, gate_w, gate_b, gla_norm_g, out_w, final_g)


def kernel(x, c, ada_w, ada_b, norm_g, even_in_w, pool_w, pool_scale, sgu_norm_g, sgu_w, sgu_b, even_out_w,
           odd_in_w, gla_gate_w, gla_gate_b, gla_norm_g, odd_out_w, final_g):
    depth, d, _ = ada_w.shape
    assert depth == 2 and even_in_w.shape[0] == 1 and odd_in_w.shape[0] == 1
    b = x.shape[0]
    ada4 = _ada_call(c, ada_w, ada_b).reshape(depth, b, 3, d)

    x, in_wt, odd_out = _even_call(
        x, ada4, 0, norm_g[0:1], even_in_w[0].astype(BF16), pool_w[0].astype(BF16), pool_scale[0:1],
        sgu_norm_g[0:1], sgu_w[0], sgu_b[0].T, even_out_w[0].astype(BF16),
        [jnp.swapaxes(odd_in_w[0], 0, 1), odd_out_w[0]])
    return _odd_call(
        x, ada4, 1, norm_g[1:2], in_wt, gla_gate_w[0], gla_gate_b[0:1], gla_norm_g[0].reshape(1, -1),
        odd_out, final_g.reshape(1, -1))
```

```python
import functools

import jax
import jax.numpy as jnp
from jax import lax
from jax.experimental import pallas as pl
from jax.experimental.pallas import tpu as pltpu

F32 = jnp.float32
BF16 = jnp.bfloat16

EPS = 1e-6
LOG2E = 1.4426950408889634
POOL_WINDOWS = (2, 4, 8, 16)
POOL_GROUP = 256
POOL_HALO = 32
SGU_LEN = 128
SGU_HEADS = 8
GLA_HEADS = 4
GLA_DK = 128
GLA_DV = 256
GLA_CHUNK = 64
GLA_RANK = 16
GLA_TAU = 16.0
SB_HEADS = 8
SB_DH = 128
SB_GROUP = 4
SB_STAGES = 4
SB_DEAD_CARRY = 105.0
SB_NO_BLOCK = 1.0e4

TS_EVEN = 512
TS_ODD = 256
VMEM_LIMIT = 56 * 1024 * 1024


def _dot(a, b):
    return jnp.dot(a, b, preferred_element_type=F32)


def _dot_nt(a, b):
    return lax.dot_general(a, b, (((1,), (1,)), ((), ())), preferred_element_type=F32)


def _split2(x):
    hi = x.astype(BF16)
    lo = (x - hi.astype(F32)).astype(BF16)
    return hi, lo


def _sigmoid(x):
    return 1.0 / (1.0 + jnp.exp(-x))


def _silu(x):
    return x * _sigmoid(x)


def _softplus(x):
    return jnp.maximum(x, 0.0) + jnp.log(1.0 + jnp.exp(-jnp.abs(x)))


def _rms(x, g):
    return x * lax.rsqrt(jnp.mean(x * x, axis=-1, keepdims=True) + EPS) * g


def _resident(shape):
    n = len(shape)
    return pl.BlockSpec(shape, lambda *_: (0,) * n, pipeline_mode=pl.Buffered(1))


def _rider_blocks(rows, steps):
    br = -(-rows // steps)
    br = -(-br // 16) * 16
    return br, -(-rows // br)


def _ada_kernel(c_ref, w_ref, b_ref, o_ref):
    sc = _silu(c_ref[...])
    o_ref[0] = jnp.dot(sc, w_ref[0], preferred_element_type=F32,
                       precision=lax.Precision.HIGHEST) + b_ref[0]


def _ada_call(c, ada_w, ada_b):
    depth, d, d3 = ada_w.shape
    b = c.shape[0]
    tn = 1024
    return pl.pallas_call(
        _ada_kernel,
        grid=(depth, d3 // tn),
        in_specs=[
            pl.BlockSpec((b, d), lambda i, j: (0, 0)),
            pl.BlockSpec((1, d, tn), lambda i, j: (i, 0, j)),
            pl.BlockSpec((1, 1, tn), lambda i, j: (i, 0, j)),
        ],
        out_specs=pl.BlockSpec((1, b, tn), lambda i, j: (i, 0, j)),
        out_shape=jax.ShapeDtypeStruct((depth, b, d3), F32),
        compiler_params=pltpu.CompilerParams(
            dimension_semantics=("arbitrary", "arbitrary"), vmem_limit_bytes=VMEM_LIMIT),
        name="ada",
    )(c, ada_w, ada_b.reshape(depth, 1, d3))


def _even_kernel(x_ref, ada_ref, ng_ref, win_ref, pw_ref, ps_ref, sg_ref, sw_ref, sbt_ref, wout_ref,
                 *rest, ts, rider_rows):
    n_rider = len(rider_rows)
    rider_in, o_ref, rider_out = rest[:n_rider], rest[n_rider], rest[n_rider + 1:2 * n_rider + 1]
    abuf, s2buf, s4buf, s8buf, s16buf = rest[2 * n_rider + 1:]
    s = pl.program_id(1)

    step = pl.program_id(0) * pl.num_programs(1) + s
    for src, dst, rows in zip(rider_in, rider_out, rider_rows):
        br = src.shape[0]
        blk = jnp.minimum(step, -(-rows // br) - 1)
        row = blk * br + lax.broadcasted_iota(jnp.int32, src.shape, 0)
        dst[...] = jnp.where(row < rows, src[...], 0.0).astype(dst.dtype)

    d = x_ref.shape[-1]
    hl = POOL_HALO
    g = POOL_GROUP

    @pl.when(s == 0)
    def _():
        abuf[0:hl, :] = jnp.zeros((hl, d), F32)

    @pl.when(s > 0)
    def _():
        abuf[0:hl, :] = abuf[ts:ts + hl, :]

    shift = ada_ref[0, 0, 0:1, :]
    scale = ada_ref[0, 0, 1:2, :]
    h = (_rms(x_ref[0], ng_ref[...] * (1.0 + scale)) + shift).astype(BF16)

    a = _dot(h, win_ref[:, 0:d])
    abuf[hl:hl + ts, :] = a
    u = _dot(h, win_ref[:, d:2 * d])

    n2 = ts + hl - 8
    s2buf[8:8 + n2, :] = abuf[8:8 + n2, :] + abuf[7:7 + n2, :]
    n4 = ts + hl - 16
    s4buf[16:16 + n4, :] = s2buf[16:16 + n4, g:] + s2buf[14:14 + n4, g:]
    n8 = ts + hl - 24
    s8buf[24:24 + n8, :] = s4buf[24:24 + n8, g:] + s4buf[20:20 + n8, g:]
    s16buf[hl:hl + ts, :] = s8buf[hl:hl + ts, g:] + s8buf[hl - 8:hl - 8 + ts, g:]
    v = _dot(h, win_ref[:, 2 * d:3 * d])

    t_glob = s * ts + lax.broadcasted_iota(jnp.int32, (ts, 1), 0)
    win_sums = (s2buf[hl:hl + ts, 0:g], s4buf[hl:hl + ts, 0:g], s8buf[hl:hl + ts, 0:g], s16buf[hl:hl + ts, :])
    ya = []
    for gi, w in enumerate(POOL_WINDOWS):
        count = jnp.minimum(t_glob + 1, w).astype(F32)
        p = win_sums[gi] / count - a[:, gi * g:(gi + 1) * g]
        ya.append(_dot(p.astype(BF16), pw_ref[gi]))
    gate_a = _silu(_dot(h, win_ref[:, 3 * d:4 * d]))
    ya = jnp.concatenate(ya, axis=1) * ps_ref[...] * gate_a

    vn = _rms(v, sg_ref[...]).astype(BF16)
    gate_b = _silu(_dot(h, win_ref[:, 4 * d:5 * d]))
    nblk = ts // SGU_LEN
    row = lax.broadcasted_iota(jnp.int32, (SGU_LEN, SGU_LEN), 0)
    col = lax.broadcasted_iota(jnp.int32, (SGU_LEN, SGU_LEN), 1)
    hd = d // SGU_HEADS
    z_cols = []
    for hh in range(SGU_HEADS):
        wm = jnp.where(col <= row, sw_ref[hh], 0.0).astype(BF16)
        vh = jnp.concatenate(
            [vn[n * SGU_LEN:(n + 1) * SGU_LEN, hh * hd:(hh + 1) * hd] for n in range(nblk)], axis=1)
        zh = _dot(wm, vh) + sbt_ref[:, hh:hh + 1]
        z_cols.append(jnp.concatenate([zh[:, n * hd:(n + 1) * hd] for n in range(nblk)], axis=0))
    yb = u * jnp.concatenate(z_cols, axis=1) * gate_b

    y = jnp.concatenate([ya, yb], axis=1).astype(BF16)
    o_ref[0] = x_ref[0] + ada_ref[0, 0, 2:3, :] * _dot(y, wout_ref[...])


def _even_call(x, ada4, layer, norm_g, in_w, pool_w, pool_scale, sgu_norm_g, sgu_w, sgu_bt, out_w, riders):
    b, sq, d = x.shape
    ts = TS_EVEN
    g = POOL_GROUP
    rows = ts + POOL_HALO
    nj = sq // ts
    rider_blocks = [_rider_blocks(r.shape[0], b * nj) for r in riders]

    def rider_spec(r, br, nblk):
        return pl.BlockSpec((br, r.shape[1]), lambda i, j: (jnp.minimum(i * nj + j, nblk - 1), 0))

    rider_specs = [rider_spec(r, br, nblk) for r, (br, nblk) in zip(riders, rider_blocks)]
    return pl.pallas_call(
        functools.partial(_even_kernel, ts=ts, rider_rows=tuple(r.shape[0] for r in riders)),
        grid=(b, nj),
        in_specs=[
            pl.BlockSpec((1, ts, d), lambda i, j: (i, j, 0)),
            pl.BlockSpec((1, 1, 3, d), lambda i, j: (layer, i, 0, 0)),
            _resident(norm_g.shape),
            _resident(in_w.shape),
            _resident(pool_w.shape),
            _resident(pool_scale.shape),
            _resident(sgu_norm_g.shape),
            _resident(sgu_w.shape),
            _resident(sgu_bt.shape),
            _resident(out_w.shape),
        ] + rider_specs,
        out_specs=[pl.BlockSpec((1, ts, d), lambda i, j: (i, j, 0))] + rider_specs,
        out_shape=[jax.ShapeDtypeStruct(x.shape, F32)] + [
            jax.ShapeDtypeStruct((br * nblk, r.shape[1]), BF16) for r, (br, nblk) in zip(riders, rider_blocks)],
        scratch_shapes=[
            pltpu.VMEM((rows, d), F32),
            pltpu.VMEM((rows, d), F32),
            pltpu.VMEM((rows, d - g), F32),
            pltpu.VMEM((rows, d - 2 * g), F32),
            pltpu.VMEM((rows, d - 3 * g), F32),
        ],
        compiler_params=pltpu.CompilerParams(
            dimension_semantics=("arbitrary", "arbitrary"), vmem_limit_bytes=VMEM_LIMIT),
        name="even_layer",
    )(x, ada4, norm_g, in_w, pool_w, pool_scale, sgu_norm_g, sgu_w, sgu_bt, out_w, *riders)


def _odd_kernel(x_ref, ada_ref, ng_ref, wt_ref, gw_ref, gb_ref, gng_ref, wout_ref, fg_ref,
                o_ref, state, kbuf, vtbuf, qtbuf, acc, crow, ybuf, gbuf, *, ts):
    s = pl.program_id(1)

    @pl.when(s == 0)
    def _():
        state[...] = jnp.zeros(state.shape, F32)

    shift = ada_ref[0, 0, 0:1, :]
    scale = ada_ref[0, 0, 1:2, :]
    h = (_rms(x_ref[0], ng_ref[...] * (1.0 + scale)) + shift).astype(BF16)

    kw = GLA_HEADS * GLA_DK
    vw = GLA_HEADS * GLA_DV
    sbw = SB_HEADS * SB_DH
    r_gq, r_gk, r_gv = 0, kw, 2 * kw
    r_glr = r_gv + vw
    r_sq = r_glr + GLA_RANK
    r_sk, r_sv, r_gate = r_sq + sbw, r_sq + 2 * sbw, r_sq + 3 * sbw

    def proj(r0, n):
        return _dot_nt(h, wt_ref[r0:r0 + n, :])

    sb_scale = SB_DH ** -0.5
    s0 = pl.multiple_of(s * ts, ts)

    sq = proj(r_sq, sbw) * sb_scale
    sk = proj(r_sk, sbw).astype(BF16)
    sv = proj(r_sv, sbw)
    for hh in range(SB_HEADS):
        hs = slice(hh * SB_DH, (hh + 1) * SB_DH)
        qtbuf[hh] = sq[:, hs].T.astype(BF16)
        kbuf[hh, pl.ds(s0, ts), :] = sk[:, hs]
        vtbuf[hh, s] = sv[:, hs].T.astype(BF16)

    nchunk = ts // GLA_CHUNK
    heads = range(GLA_HEADS)
    kss = [slice(hh * GLA_DK, (hh + 1) * GLA_DK) for hh in heads]
    vss = [slice(hh * GLA_DV, (hh + 1) * GLA_DV) for hh in heads]
    r = lax.broadcasted_iota(jnp.int32, (ts, ts), 0)
    cc = lax.broadcasted_iota(jnp.int32, (ts, ts), 1)
    same_chunk_causal = (cc <= r) & (cc // GLA_CHUNK == r // GLA_CHUNK)
    gla = {}

    def gla_decay_input():
        glr = proj(r_glr, GLA_RANK)
        gla["gq"] = proj(r_gq, kw) * (GLA_DK ** -0.5)
        glr_hi, glr_lo = _split2(glr)
        gw_hi, gw_lo = _split2(gw_ref[...])
        pre = _dot(jnp.concatenate([glr_hi, glr_lo, glr_hi], axis=1),
                   jnp.concatenate([gw_hi, gw_hi, gw_lo], axis=0)) + gb_ref[...]
        gla["la"] = -_softplus(-pre) * (1.0 / GLA_TAU)

    def gla_decay_sums():
        gla["gk"] = proj(r_gk, kw)
        tri = jnp.where(same_chunk_causal, 1.0, 0.0).astype(BF16)
        la_hi, la_lo = _split2(gla["la"])
        gla["bcum"] = _dot(jnp.concatenate([tri, tri], axis=1), jnp.concatenate([la_hi, la_lo], axis=0))
        gla["gv"] = proj(r_gv, vw).astype(BF16)

    def gla_scores():
        gq, gk, bcum, gv_b = gla["gq"], gla["gk"], gla["bcum"], gla["gv"]
        gla["q_dec"] = q_dec = (gq * jnp.exp(bcum)).astype(BF16)
        k_inv = (gk * jnp.exp(-bcum)).astype(BF16)
        b_last = jnp.concatenate(
            [jnp.broadcast_to(bcum[(ci + 1) * GLA_CHUNK - 1:(ci + 1) * GLA_CHUNK, :], (GLA_CHUNK, kw))
             for ci in range(nchunk)], axis=0)
        k_end = gk * jnp.exp(b_last - bcum)
        gla["zs"] = [_dot_nt(q_dec[:, kss[hh]], k_inv[:, kss[hh]]) for hh in heads]
        lane_chunk = lax.broadcasted_iota(jnp.int32, (GLA_DK, ts), 1) // GLA_CHUNK
        gla["us"], gla["bcts"] = [], []
        for hh in heads:
            ket = k_end[:, kss[hh]].T
            gla["bcts"].append(bcum[:, kss[hh]].T)
            lhs = jnp.concatenate(
                [jnp.where(lane_chunk == ci, ket, 0.0).astype(BF16) for ci in range(nchunk)], axis=0)
            gla["us"].append(_dot(lhs, gv_b[:, vss[hh]]))

    def gla_outputs():
        for hh in heads:
            att = jnp.where(same_chunk_causal, gla["zs"][hh], 0.0).astype(BF16)
            st = state[hh]
            starts = []
            for ci in range(nchunk):
                starts.append(st.astype(BF16))
                last = (ci + 1) * GLA_CHUNK - 1
                st = (jnp.exp(gla["bcts"][hh][:, last:last + 1]) * st
                      + gla["us"][hh][ci * GLA_DK:(ci + 1) * GLA_DK, :])
            state[hh] = st
            qh = gla["q_dec"][:, kss[hh]]
            o = _dot(att, gla["gv"][:, vss[hh]]) + jnp.concatenate(
                [_dot(qh[ci * GLA_CHUNK:(ci + 1) * GLA_CHUNK, :], starts[ci]) for ci in range(nchunk)], axis=0)
            ybuf[:, vss[hh]] = _rms(o, gng_ref[:, vss[hh]])

    groups = [tuple(range(g, g + SB_GROUP)) for g in range(0, SB_HEADS, SB_GROUP)]
    n_gate = SB_STAGES * len(groups) - 4
    assert n_gate > 0 and (vw + sbw) % n_gate == 0
    gate_chunk = (vw + sbw) // n_gate

    def gate_filler(ci):
        def run():
            lo = ci * gate_chunk
            gbuf[:, lo:lo + gate_chunk] = _silu(proj(r_gate + lo, gate_chunk))
        return run

    fillers = [gla_decay_input, gla_decay_sums, gla_scores, gla_outputs] + [gate_filler(ci) for ci in range(n_gate)]
    assert len(fillers) == SB_STAGES * len(groups)

    jr = lax.broadcasted_iota(jnp.int32, (ts, ts), 0)
    jc = lax.broadcasted_iota(jnp.int32, (ts, ts), 1)
    later = jnp.where(jc > jr, 1.0, 0.0).astype(BF16)
    valid = jr < jc

    def sb_blocks(heads, kbs, fresh, penalty=None, between=()):
        between = list(between)

        def stage_done():
            if between:
                between.pop(0)()

        zs = [[_dot(kbuf[hh, pl.ds(pl.multiple_of(kb * ts, ts), ts), :], qtbuf[hh]) for hh in heads] for kb in kbs]
        stage_done()
        sps, lbetas, firsts = [], [], []
        for bi, zb in enumerate(zs):
            sps.append([])
            lbetas.append([])
            firsts.append([])
            for z in zb:
                lg = jnp.log(1.0 + jnp.exp2(jnp.abs(z) * (-LOG2E)))
                sp = jnp.maximum(z, 0.0) + lg
                if fresh and bi == 0:
                    sp = jnp.where(valid, sp, 0.0)
                sps[bi].append(sp.astype(BF16))
                firsts[bi].append(sp[0:1, :])
                lbetas[bi].append(jnp.minimum(z, 0.0) - lg)
        stage_done()
        sums = [[_dot(later, sp) for sp in spb] for spb in sps]
        stage_done()
        ws, carries = [], []
        for hi, hh in enumerate(heads):
            carry = None if fresh else crow[hh]
            ws.append([])
            for bi in range(len(kbs)):
                xarg = lbetas[bi][hi] - sums[bi][hi]
                if carry is not None:
                    xarg = xarg - carry
                w = jnp.exp(xarg)
                if fresh and bi == 0:
                    w = jnp.where(valid, w, 0.0)
                ws[hi].append(w.astype(BF16))
                total = sums[bi][hi][0:1, :] + firsts[bi][hi]
                carry = total if carry is None else carry + total
                if penalty is not None and bi == 0:
                    carry = carry + penalty
            carries.append(carry)
        stage_done()
        for hi, hh in enumerate(heads):
            contrib = _dot(jnp.concatenate([vtbuf[hh, kb] for kb in kbs], axis=1),
                           jnp.concatenate(ws[hi], axis=0))
            acc[hh] = contrib if fresh else acc[hh] + contrib
            crow[hh] = carries[hi]

    def any_alive():
        low = crow[0]
        for hh in range(1, SB_HEADS):
            low = jnp.minimum(low, crow[hh])
        return (jnp.min(low) < SB_DEAD_CARRY).astype(jnp.int32)

    penalty = jnp.where(s > 0, 0.0, SB_NO_BLOCK)
    for gi, heads in enumerate(groups):
        sb_blocks(heads, (s, jnp.maximum(s - 1, 0)), True, penalty, fillers[gi * SB_STAGES:(gi + 1) * SB_STAGES])

    def more_blocks(state_):
        i, alive = state_
        return jnp.logical_and(i < s, alive > 0)

    def key_block(state_):
        i, _ = state_
        for heads in groups:
            sb_blocks(heads, (s - 1 - i,), False)
        return i + 1, any_alive()

    @pl.when(s > 1)
    def _():
        lax.while_loop(more_blocks, key_block, (jnp.int32(1), any_alive()))

    for hh in range(SB_HEADS):
        ybuf[:, vw + hh * SB_DH:vw + (hh + 1) * SB_DH] = acc[hh].T

    y = (ybuf[...] * gbuf[...]).astype(BF16)
    x1 = x_ref[0] + ada_ref[0, 0, 2:3, :] * _dot(y, wout_ref[...])
    o_ref[0] = _rms(x1, fg_ref[...])


def _odd_call(x, ada4, layer, norm_g, in_wt, gate_w, gate_b, gla_norm_g, out_w, final_g):
    b, sq, d = x.shape
    ts = TS_ODD
    return pl.pallas_call(
        functools.partial(_odd_kernel, ts=ts),
        grid=(b, sq // ts),
        in_specs=[
            pl.BlockSpec((1, ts, d), lambda i, j: (i, j, 0)),
            pl.BlockSpec((1, 1, 3, d), lambda i, j: (layer, i, 0, 0)),
            _resident(norm_g.shape),
            _resident(in_wt.shape),
            _resident(gate_w.shape),
            _resident(gate_b.shape),
            _resident(gla_norm_g.shape),
            _resident(out_w.shape),
            _resident(final_g.shape),
        ],
        out_specs=pl.BlockSpec((1, ts, d), lambda i, j: (i, j, 0)),
        out_shape=jax.ShapeDtypeStruct(x.shape, F32),
        scratch_shapes=[
            pltpu.VMEM((GLA_HEADS, GLA_DK, GLA_DV), F32),
            pltpu.VMEM((SB_HEADS, sq, SB_DH), BF16),
            pltpu.VMEM((SB_HEADS, sq // ts, SB_DH, ts), BF16),
            pltpu.VMEM((SB_HEADS, SB_DH, ts), BF16),
            pltpu.VMEM((SB_HEADS, SB_DH, ts), F32),
            pltpu.VMEM((SB_HEADS, 1, ts), F32),
            pltpu.VMEM((ts, 2 * d), F32),
            pltpu.VMEM((ts, 2 * d), F32),
        ],
        compiler_params=pltpu.CompilerParams(
            dimension_semantics=("arbitrary", "arbitrary"), vmem_limit_bytes=VMEM_LIMIT),
        name="odd_layer",
    )(x, ada4, norm_g, in_wt, gate_w, gate_b, gla_norm_g, out_w, final_g)


def kernel(x, c, ada_w, ada_b, norm_g, even_in_w, pool_w, pool_scale, sgu_norm_g, sgu_w, sgu_b, even_out_w,
           odd_in_w, gla_gate_w, gla_gate_b, gla_norm_g, odd_out_w, final_g):
    depth, d, _ = ada_w.shape
    assert depth == 2 and even_in_w.shape[0] == 1 and odd_in_w.shape[0] == 1
    b = x.shape[0]
    ada4 = _ada_call(c, ada_w, ada_b).reshape(depth, b, 3, d)

    x, in_wt, odd_out = _even_call(
        x, ada4, 0, norm_g[0:1], even_in_w[0].astype(BF16), pool_w[0].astype(BF16), pool_scale[0:1],
        sgu_norm_g[0:1], sgu_w[0], sgu_b[0].T, even_out_w[0].astype(BF16),
        [jnp.swapaxes(odd_in_w[0], 0, 1), odd_out_w[0]])
    return _odd_call(
        x, ada4, 1, norm_g[1:2], in_wt, gla_gate_w[0], gla_gate_b[0:1], gla_norm_g[0].reshape(1, -1),
        odd_out, final_g.reshape(1, -1))
```
